```python
import math
import jax, jax.numpy as jnp
from jax import lax
import numpy as np

D_MODEL = 1024
BATCH = 4
SEQ = 8192
DEPTH = 2
DEC_BATCH = 128
DEC_SEQ = 1
PAST_LEN = 16384
PAGE_SIZE = 128

HEAD_DIM = 64
GROUP_W = D_MODEL // 4
CONV_W = GROUP_W
CONV_K = 3
RWKV_W = GROUP_W
RWKV_HEADS = RWKV_W // HEAD_DIM
LORA_W = 64
LORA_A = 64
RWKV_SHIFT_W = 3 * RWKV_W + LORA_W + LORA_A
MLA_HEADS = GROUP_W // HEAD_DIM
Q_RANK = 256
KV_RANK = 128
NOPE_DIM = 64
ROPE_DIM = 32
V_DIM = HEAD_DIM
MLA_W = MLA_HEADS * V_DIM
N_MEM = 256
MEM_HEADS = 4
MEM_W = MEM_HEADS * HEAD_DIM
D_MIX = CONV_W + RWKV_W + MLA_W + MEM_W
IN_SPLITS = (CONV_W, CONV_W, CONV_W, CONV_W, RWKV_SHIFT_W, RWKV_W, Q_RANK, KV_RANK, ROPE_DIM, MLA_W, MEM_W, MEM_W)
D_IN = 4 * CONV_W + RWKV_SHIFT_W + RWKV_W + Q_RANK + KV_RANK + ROPE_DIM + MLA_W + 2 * MEM_W
RWKV_SPLITS = (RWKV_W, RWKV_W, RWKV_W, LORA_W, LORA_A)
Q_BLOCK = 128
RMS_EPS = 1e-6
LNX_EPS = 64e-5
ROPE_BASE = 10000.0
DECAY_SCALE = 0.6065306597
MLA_SCALE = (NOPE_DIM + ROPE_DIM) ** -0.5
NEG_INF = -1e30

kernel_name = "hymba_conv_rwkv7_mla_memory_step"


def _split(x, sizes):
    idx = np.cumsum(np.array(sizes))[:-1].tolist()
    return jnp.split(x, idx, axis=-1)


def rms_norm(x, g):
    xf = x.astype(jnp.float32)
    y = xf * lax.rsqrt(jnp.mean(xf * xf, axis=-1, keepdims=True) + RMS_EPS)
    return (y * g.astype(jnp.float32)).astype(x.dtype)


def rope(x, pos):
    half = ROPE_DIM // 2
    inv = jnp.power(ROPE_BASE, -jnp.arange(half, dtype=jnp.float32) / half)
    ang = pos.astype(jnp.float32)[:, None] * inv[None, :]
    ang = ang.reshape(ang.shape[0], *([1] * (x.ndim - 3)), half)
    cos = jnp.cos(ang).astype(x.dtype)
    sin = jnp.sin(ang).astype(x.dtype)
    x1, x2 = x[..., :half], x[..., half:]
    return jnp.concatenate([x1 * cos - x2 * sin, x2 * cos + x1 * sin], axis=-1)


def short_conv(c_b, c_c, c_x, buf, w_conv):
    u = c_c * c_x
    S = u.shape[1]
    ext = jnp.concatenate([buf.astype(u.dtype), u], axis=1)
    y = ext[:, 0:S] * w_conv[0]
    for i in range(1, CONV_K):
        y = y + ext[:, i:i + S] * w_conv[i]
    return c_b * y, ext[:, S:]


def rwkv7_mix(proj_sh, prev_row, state0, p):
    B, S, _ = proj_sh.shape
    prev = jnp.concatenate([prev_row[:, None].astype(proj_sh.dtype), proj_sh[:, :-1]], axis=1)
    mixed = proj_sh + (prev - proj_sh) * p['mu']
    r, k, v, wl, al = _split(mixed, RWKV_SPLITS)
    log_w = -DECAY_SCALE * jax.nn.sigmoid(p['w0'] + jnp.tanh(wl) @ p['w2'])
    a = jax.nn.sigmoid(p['a0'] + al @ p['a2'])
    heads = lambda t: t.reshape(B, S, RWKV_HEADS, HEAD_DIM)
    kk = heads(k * p['k_k']).astype(jnp.float32)
    kk = kk * lax.rsqrt(jnp.sum(kk * kk, axis=-1, keepdims=True) + 1e-12)
    k = k * (1 + (a - 1) * p['k_a'])
    r_h, k_h, v_h, a_h = heads(r), heads(k), heads(v), heads(a)
    w_h = jnp.exp(heads(log_w).astype(jnp.float32))
    xs = tuple(jnp.moveaxis(t.astype(jnp.float32), 1, 0) for t in (r_h, k_h, v_h, w_h, kk, a_h))

    def step(St, inp):
        r_t, k_t, v_t, w_t, kk_t, a_t = inp
        sa = jnp.einsum('bhij,bhj->bhi', St, kk_t)
        St = (St * w_t[:, :, None, :] - sa[..., None] * (kk_t * a_t)[:, :, None, :]
              + v_t[..., None] * k_t[:, :, None, :])
        return St, jnp.einsum('bhij,bhj->bhi', St, r_t)

    S_final, y = lax.scan(step, state0.astype(jnp.float32), xs)
    y = jnp.moveaxis(y, 0, 1)
    mean = jnp.mean(y, axis=-1, keepdims=True)
    var = jnp.mean(jnp.square(y - mean), axis=-1, keepdims=True)
    y = ((y - mean) * lax.rsqrt(var + LNX_EPS)).reshape(B, S, RWKV_W)
    y = y * p['lnx_g'].astype(jnp.float32) + p['lnx_b'].astype(jnp.float32)
    bonus = jnp.sum(r_h * k_h * p['r_k'], axis=-1, keepdims=True) * v_h
    y = y.astype(proj_sh.dtype) + bonus.reshape(B, S, RWKV_W)
    return y, proj_sh[:, -1], S_final


def mem_kv(mem, g, wk, wv):
    B, M, _ = mem.shape
    mh = rms_norm(mem, g)
    k = (mh @ wk).reshape(B, M, MEM_HEADS, HEAD_DIM)
    v = (mh @ wv).reshape(B, M, MEM_HEADS, HEAD_DIM)
    return k, v


def mem_attend(q, mem_k, mem_v):
    B, S, _ = q.shape
    qh = q.reshape(B, S, MEM_HEADS, HEAD_DIM)
    s = jnp.einsum('bshd,bmhd->bhsm', qh, mem_k).astype(jnp.float32) * (HEAD_DIM ** -0.5)
    pr = jax.nn.softmax(s, axis=-1).astype(q.dtype)
    return jnp.einsum('bhsm,bmhd->bshd', pr, mem_v).reshape(B, S, MEM_W)


def prompt_attend(q_lat, q_rope, ckv, krope):
    B, S, H, R = q_lat.shape
    nb = S // Q_BLOCK
    blk = lambda t: jnp.moveaxis(t.reshape(B, nb, Q_BLOCK, *t.shape[2:]), 1, 0)
    kpos = jnp.arange(S)

    def one_block(args):
        i, ql, qr = args
        s = jnp.einsum('bqhr,bkr->bhqk', ql, ckv) + jnp.einsum('bqhd,bkd->bhqk', qr, krope)
        qpos = i * Q_BLOCK + jnp.arange(Q_BLOCK)
        s = jnp.where(kpos[None, :] <= qpos[:, None], s.astype(jnp.float32) * MLA_SCALE, NEG_INF)
        pr = jax.nn.softmax(s, axis=-1).astype(ckv.dtype)
        return jnp.einsum('bhqk,bkr->bqhr', pr, ckv)

    o = lax.map(one_block, (jnp.arange(nb), blk(q_lat), blk(q_rope)))
    return jnp.moveaxis(o, 0, 1).reshape(B, S, H, R)


def make_sample_attend(ckv_past, krope_past):
    def attend(q_lat, q_rope, ckv_new, krope_new):
        T = q_lat.shape[1]
        P = ckv_past.shape[1]
        s_past = jnp.einsum('bqhr,bkr->bhqk', q_lat, ckv_past) + jnp.einsum('bqhd,bkd->bhqk', q_rope, krope_past)
        s_new = jnp.einsum('bqhr,bkr->bhqk', q_lat, ckv_new) + jnp.einsum('bqhd,bkd->bhqk', q_rope, krope_new)
        causal = jnp.tril(jnp.ones((T, T), dtype=bool))
        s_new = jnp.where(causal, s_new.astype(jnp.float32) * MLA_SCALE, NEG_INF)
        s = jnp.concatenate([s_past.astype(jnp.float32) * MLA_SCALE, s_new], axis=-1)
        pr = jax.nn.softmax(s, axis=-1).astype(ckv_new.dtype)
        return (jnp.einsum('bhqk,bkr->bqhr', pr[..., :P], ckv_past)
                + jnp.einsum('bhqk,bkr->bqhr', pr[..., P:], ckv_new))
    return attend


def trunk_layer(x, pos, conv_buf, shift_prev, rwkv_state, mem_k, mem_v, attend, p):
    B, S, _ = x.shape
    h = rms_norm(x, p['pre_g'])
    proj = jnp.einsum('bsd,de->bse', h, p['w_in'])
    (c_b, c_c, c_x, c_gate, r_sh, r_gate, q_down, ckv_raw, kr_raw,
     mla_gate, mem_q, mem_gate) = _split(proj, IN_SPLITS)
    conv_out, conv_buf_new = short_conv(c_b, c_c, c_x, conv_buf, p['conv_w'])
    rwkv_out, shift_new, rwkv_state_new = rwkv7_mix(r_sh, shift_prev, rwkv_state, p)
    q = jnp.einsum('bsr,re->bse', rms_norm(q_down, p['q_norm_g']), p['w_uq'])
    q = q.reshape(B, S, MLA_HEADS, NOPE_DIM + ROPE_DIM)
    q_nope, q_rope = q[..., :NOPE_DIM], rope(q[..., NOPE_DIM:], pos)
    ckv = rms_norm(ckv_raw, p['kv_norm_g'])
    krope = rope(kr_raw, pos)
    q_lat = jnp.einsum('bshd,rhd->bshr', q_nope, p['w_uk'])
    o_lat = attend(q_lat, q_rope, ckv, krope)
    mla_out = jnp.einsum('bshr,rhd->bshd', o_lat, p['w_uv']).reshape(B, S, MLA_W)
    mem_out = mem_attend(mem_q, mem_k, mem_v)
    mixed = jnp.concatenate([conv_out * jax.nn.silu(c_gate), rwkv_out * jax.nn.silu(r_gate),
                             mla_out * jax.nn.silu(mla_gate), mem_out * jax.nn.silu(mem_gate)], axis=-1)
    x = x + rms_norm(jnp.einsum('bse,ed->bsd', mixed, p['w_out']), p['post_g'])
    return x, conv_buf_new, shift_new, rwkv_state_new, ckv, krope


def setup_inputs(seed: int = 0) -> dict:
    key = jax.random.key(seed)
    ks = iter(jax.random.split(key, 48))
    nrm = lambda shape, scale=1.0: jax.random.normal(next(ks), shape, jnp.float32) * scale
    gain = lambda shape: 1.0 + nrm(shape, 0.05)
    n_pages = PAST_LEN // PAGE_SIZE
    n_used = DEC_BATCH * n_pages
    n_phys = n_used + max(1, n_used // 4)
    x_prompt = nrm((BATCH, SEQ, D_MODEL))
    x_sample = nrm((DEC_BATCH, DEC_SEQ, D_MODEL))
    cache_ckv = nrm((DEPTH, n_phys, PAGE_SIZE, KV_RANK))
    cache_krope = nrm((DEPTH, n_phys, PAGE_SIZE, ROPE_DIM))
    cache_mem_k = nrm((DEPTH, DEC_BATCH, N_MEM, MEM_HEADS, HEAD_DIM))
    cache_mem_v = nrm((DEPTH, DEC_BATCH, N_MEM, MEM_HEADS, HEAD_DIM))
    state_conv = nrm((DEPTH, DEC_BATCH, CONV_K - 1, CONV_W))
    state_rwkv_shift = nrm((DEPTH, DEC_BATCH, RWKV_SHIFT_W))
    state_rwkv = nrm((DEPTH, DEC_BATCH, RWKV_HEADS, HEAD_DIM, HEAD_DIM), 0.1)
    page_table = jax.random.permutation(next(ks), n_phys)[:n_used].reshape(DEC_BATCH, n_pages).astype(jnp.int32)
    mem_prompt = nrm((BATCH, N_MEM, D_MODEL))
    return {
        'x_prompt': x_prompt,
        'x_sample': x_sample,
        'cache_ckv': cache_ckv,
        'cache_krope': cache_krope,
        'cache_mem_k': cache_mem_k,
        'cache_mem_v': cache_mem_v,
        'state_conv': state_conv,
        'state_rwkv_shift': state_rwkv_shift,
        'state_rwkv': state_rwkv,
        'page_table': page_table,
        'mem_prompt': mem_prompt,
        'pre_norm_g': gain((DEPTH, D_MODEL)),
        'post_norm_g': gain((DEPTH, D_MODEL)),
        'w_in': nrm((DEPTH, D_MODEL, D_IN), D_MODEL ** -0.5),
        'w_out': nrm((DEPTH, D_MIX, D_MODEL), D_MIX ** -0.5),
        'conv_w': nrm((DEPTH, CONV_K, CONV_W), CONV_K ** -0.5),
        'rwkv_mu': jax.random.uniform(next(ks), (DEPTH, RWKV_SHIFT_W), jnp.float32),
        'rwkv_w0': nrm((DEPTH, RWKV_W), 0.5),
        'rwkv_w2': nrm((DEPTH, LORA_W, RWKV_W), 0.1),
        'rwkv_a0': nrm((DEPTH, RWKV_W), 0.1),
        'rwkv_a2': nrm((DEPTH, LORA_A, RWKV_W), 0.1),
        'rwkv_k_k': 0.85 + nrm((DEPTH, RWKV_W), 0.05),
        'rwkv_k_a': gain((DEPTH, RWKV_W)),
        'rwkv_r_k': nrm((DEPTH, RWKV_HEADS, HEAD_DIM), 0.1),
        'rwkv_lnx_g': gain((DEPTH, RWKV_W)),
        'rwkv_lnx_b': nrm((DEPTH, RWKV_W), 0.01),
        'mla_q_norm_g': gain((DEPTH, Q_RANK)),
        'mla_w_uq': nrm((DEPTH, Q_RANK, MLA_HEADS * (NOPE_DIM + ROPE_DIM)), Q_RANK ** -0.5),
        'mla_kv_norm_g': gain((DEPTH, KV_RANK)),
        'mla_w_uk': nrm((DEPTH, KV_RANK, MLA_HEADS, NOPE_DIM), KV_RANK ** -0.5),
        'mla_w_uv': nrm((DEPTH, KV_RANK, MLA_HEADS, V_DIM), KV_RANK ** -0.5),
        'mem_norm_g': gain((DEPTH, D_MODEL)),
        'w_mem_k': nrm((DEPTH, D_MODEL, MEM_W), D_MODEL ** -0.5),
        'w_mem_v': nrm((DEPTH, D_MODEL, MEM_W), D_MODEL ** -0.5),
    }


def reference(x_prompt, x_sample, cache_ckv, cache_krope, cache_mem_k, cache_mem_v, state_conv,
              state_rwkv_shift, state_rwkv, page_table, mem_prompt, pre_norm_g, post_norm_g, w_in, w_out,
              conv_w, rwkv_mu, rwkv_w0, rwkv_w2, rwkv_a0, rwkv_a2, rwkv_k_k, rwkv_k_a, rwkv_r_k, rwkv_lnx_g,
              rwkv_lnx_b, mla_q_norm_g, mla_w_uq, mla_kv_norm_g, mla_w_uk, mla_w_uv, mem_norm_g, w_mem_k, w_mem_v):
    Bp, Sp, _ = x_prompt.shape
    Bs, Ts, _ = x_sample.shape
    pos_p = jnp.arange(Sp, dtype=jnp.int32)
    pos_s = PAST_LEN + jnp.arange(Ts, dtype=jnp.int32)
    conv0 = jnp.zeros((Bp, CONV_K - 1, CONV_W), x_prompt.dtype)
    shift0 = jnp.zeros((Bp, RWKV_SHIFT_W), x_prompt.dtype)
    rstate0 = jnp.zeros((Bp, RWKV_HEADS, HEAD_DIM, HEAD_DIM), jnp.float32)
    xp, xs = x_prompt, x_sample
    ckv_p, ckv_s, kr_p, kr_s, mk_p, mv_p = [], [], [], [], [], []
    cv_p, cv_s, sh_p, sh_s, st_p, st_s = [], [], [], [], [], []
    for l in range(DEPTH):
        p = {'pre_g': pre_norm_g[l], 'post_g': post_norm_g[l], 'w_in': w_in[l], 'w_out': w_out[l],
             'conv_w': conv_w[l], 'mu': rwkv_mu[l], 'w0': rwkv_w0[l], 'w2': rwkv_w2[l], 'a0': rwkv_a0[l],
             'a2': rwkv_a2[l], 'k_k': rwkv_k_k[l], 'k_a': rwkv_k_a[l], 'r_k': rwkv_r_k[l],
             'lnx_g': rwkv_lnx_g[l], 'lnx_b': rwkv_lnx_b[l], 'q_norm_g': mla_q_norm_g[l], 'w_uq': mla_w_uq[l],
             'kv_norm_g': mla_kv_norm_g[l], 'w_uk': mla_w_uk[l], 'w_uv': mla_w_uv[l]}
        mk, mv = mem_kv(mem_prompt, mem_norm_g[l], w_mem_k[l], w_mem_v[l])
        xp, cb, sh, st, ckv, kr = trunk_layer(xp, pos_p, conv0, shift0, rstate0, mk, mv, prompt_attend, p)
        ckv_p.append(ckv); kr_p.append(kr); mk_p.append(mk); mv_p.append(mv)
        cv_p.append(cb); sh_p.append(sh); st_p.append(st)
        ckv_past = cache_ckv[l, page_table].reshape(Bs, -1, KV_RANK)
        kr_past = cache_krope[l, page_table].reshape(Bs, -1, ROPE_DIM)
        xs, cb, sh, st, ckv, kr = trunk_layer(xs, pos_s, state_conv[l], state_rwkv_shift[l], state_rwkv[l],
                                              cache_mem_k[l], cache_mem_v[l],
                                              make_sample_attend(ckv_past, kr_past), p)
        ckv_s.append(ckv); kr_s.append(kr); cv_s.append(cb); sh_s.append(sh); st_s.append(st)
    return (xp, xs, jnp.stack(ckv_p), jnp.stack(ckv_s), jnp.stack(kr_p), jnp.stack(kr_s),
            jnp.stack(mk_p), jnp.stack(mv_p), jnp.stack(cv_p), jnp.stack(cv_s),
            jnp.stack(sh_p), jnp.stack(sh_s), jnp.stack(st_p), jnp.stack(st_s))
```

```python
import functools

import jax
import jax.numpy as jnp
import numpy as np
from jax import lax
from jax.experimental import pallas as pl
from jax.experimental.pallas import tpu as pltpu

F32 = jnp.float32
BF16 = jnp.bfloat16
HIGHEST = lax.Precision.HIGHEST

HEAD_DIM = 64
N_HEADS = 4
GROUP_W = N_HEADS * HEAD_DIM
CONV_K = 3
LORA_W = 64
RWKV_SHIFT_W = 3 * GROUP_W + 2 * LORA_W
Q_RANK = 256
KV_RANK = 128
NOPE_DIM = 64
ROPE_DIM = 32
QK_DIM = KV_RANK + ROPE_DIM
RMS_EPS = 1e-6
LNX_EPS = 64e-5
ROPE_BASE = 10000.0
DECAY_SCALE = 0.6065306597
MLA_SCALE = (NOPE_DIM + ROPE_DIM) ** -0.5
MEM_SCALE = HEAD_DIM ** -0.5
NEG_INF = -1e30

LANES = 128
CHUNK = 64
SUB = 16
VMEM_LIMIT = 56 * 1024 * 1024

_C_CONV, _C_RSH, _C_RGATE, _C_QDOWN, _C_CKV, _C_MLAG, _C_MEMQ, _C_MEMG, _C_KR, _C_END = (
    0, 1024, 1920, 2176, 2432, 2560, 2816, 3072, 3328, 3456)


def _mm(a, b):
    return jnp.dot(a.astype(BF16), b.astype(BF16), preferred_element_type=F32)


def _mm_nt(a, b):
    return lax.dot_general(a.astype(BF16), b.astype(BF16), (((1,), (1,)), ((), ())),
                           preferred_element_type=F32)


def _mm_tn(a, b):
    return lax.dot_general(a.astype(BF16), b.astype(BF16), (((0,), (0,)), ((), ())),
                           preferred_element_type=F32)


def _mm_hi(a, b):
    return jnp.dot(a, b, preferred_element_type=F32, precision=HIGHEST)


def _sigmoid(x):
    return 1.0 / (1.0 + jnp.exp(-x))


def _silu(x):
    return x * _sigmoid(x)


def _rms(x, g):
    return x * lax.rsqrt(jnp.mean(x * x, axis=-1, keepdims=True) + RMS_EPS) * g


def _params(sem):
    return pltpu.CompilerParams(dimension_semantics=sem, vmem_limit_bytes=VMEM_LIMIT)


def _const_spec(shape):
    nd = len(shape)
    return pl.BlockSpec(shape, lambda *_: (0,) * nd)


def _tile(n, pref):
    t = min(n, pref)
    assert n % t == 0, (n, pref)
    return t


def _proj_body(decode, tiles_per_seq, pos0, *refs):
    if decode:
        (x_ref, g_ref, w_ref, cw_ref, qg_ref, wuq_ref, kvg_ref, wuk_ref, inv_ref, cb0_ref, cb1_ref,
         mconv_ref, cn0_ref, cn1_ref, rsh_ref, rgate_ref, ckv_ref, kr_ref, kcat_ref, qcat_ref,
         mlag_ref, memq_ref, memg_ref) = refs
    else:
        (x_ref, g_ref, w_ref, cw_ref, qg_ref, wuq_ref, kvg_ref, wuk_ref, inv_ref, mk_ref, mv_ref,
         mconv_ref, cst_ref, rsh_ref, rgate_ref, ckv_ref, kr_ref, kcat_ref, qcat_ref,
         mlag_ref, mmem_ref, carry_ref) = refs
    tm = x_ref.shape[0]
    t = pl.program_id(0) % tiles_per_seq
    h = _rms(x_ref[...], g_ref[...]).astype(BF16)

    def proj(lo, hi):
        return jnp.dot(h, w_ref[:, lo:hi], preferred_element_type=F32)

    cv = proj(_C_CONV, _C_RSH)
    c_b, c_c, c_x, c_g = (cv[:, i * GROUP_W:(i + 1) * GROUP_W] for i in range(4))
    u = c_c * c_x
    w0, w1, w2 = cw_ref[0:1, :], cw_ref[1:2, :], cw_ref[2:3, :]
    if decode:
        b0, b1 = cb0_ref[...], cb1_ref[...]
        y = b0 * w0 + b1 * w1 + u * w2
        cn0_ref[...] = b1
        cn1_ref[...] = u
    else:
        @pl.when(t == 0)
        def _():
            carry_ref[...] = jnp.zeros_like(carry_ref)
        p0, p1 = carry_ref[0:1, :], carry_ref[1:2, :]
        row = lax.broadcasted_iota(jnp.int32, u.shape, 0)
        u1 = jnp.where(row == 0, p1, pltpu.roll(u, 1, 0))
        u2 = jnp.where(row == 0, p0, jnp.where(row == 1, p1, pltpu.roll(u, 2, 0)))
        y = u2 * w0 + u1 * w1 + u * w2
        carry_ref[0:2, :] = u[tm - 2:tm, :]
        cst_ref[0] = u[tm - 2:tm, :]
    mconv_ref[...] = (c_b * y * _silu(c_g)).astype(mconv_ref.dtype)

    rsh_ref[...] = proj(_C_RSH, _C_RGATE)
    rgate_ref[...] = _silu(proj(_C_RGATE, _C_QDOWN))

    if decode:
        pos = jnp.full((tm, 1), pos0, F32)
    else:
        pos = (t * tm + lax.broadcasted_iota(jnp.int32, (tm, 1), 0)).astype(F32)
    ang = pos * inv_ref[...]
    cos, sin = jnp.cos(ang), jnp.sin(ang)
    first = (lax.broadcasted_iota(jnp.int32, (tm, LANES), 1) % ROPE_DIM) < (ROPE_DIM // 2)

    def rope(v):
        partner = jnp.where(first, -pltpu.roll(v, LANES - ROPE_DIM // 2, 1), pltpu.roll(v, ROPE_DIM // 2, 1))
        return v * cos + partner * sin

    qn = _rms(proj(_C_QDOWN, _C_CKV), qg_ref[...])
    q = _mm(qn, wuq_ref[...])
    q_rope = rope(q[:, GROUP_W:GROUP_W + LANES]) * MLA_SCALE
    for hh in range(N_HEADS):
        q_lat = _mm(q[:, hh * NOPE_DIM:(hh + 1) * NOPE_DIM], wuk_ref[hh]) * MLA_SCALE
        qcat_ref[0, hh, :, 0:KV_RANK] = q_lat.astype(qcat_ref.dtype)
        qcat_ref[0, hh, :, KV_RANK:QK_DIM] = q_rope[:, hh * ROPE_DIM:(hh + 1) * ROPE_DIM].astype(qcat_ref.dtype)
    ckv = _rms(proj(_C_CKV, _C_MLAG), kvg_ref[...])
    krope = rope(proj(_C_KR, _C_END))[:, 0:ROPE_DIM]
    ckv_ref[...] = ckv
    kr_ref[...] = krope
    kcat_ref[:, 0:KV_RANK] = ckv.astype(kcat_ref.dtype)
    kcat_ref[:, KV_RANK:QK_DIM] = krope.astype(kcat_ref.dtype)
    mlag_ref[...] = _silu(proj(_C_MLAG, _C_MEMQ))

    mq = proj(_C_MEMQ, _C_MEMG) * MEM_SCALE
    mg = _silu(proj(_C_MEMG, _C_KR))
    if decode:
        memq_ref[...] = mq
        memg_ref[...] = mg
    else:
        mk, mv = mk_ref[0], mv_ref[0]
        head = lax.broadcasted_iota(jnp.int32, (tm, GROUP_W), 1) // HEAD_DIM
        acc = jnp.zeros((tm, GROUP_W), F32)
        for hh in range(N_HEADS):
            s = _mm_nt(jnp.where(head == hh, mq, 0.0), mk)
            p = jnp.exp(s - jnp.max(s, axis=-1, keepdims=True))
            p = p / jnp.sum(p, axis=-1, keepdims=True)
            acc = acc + jnp.where(head == hh, _mm(p, mv), 0.0)
        mmem_ref[...] = (acc * mg).astype(mmem_ref.dtype)


def _proj_call(x, lw, *, decode, seq_len, pos0=0.0, conv_state=None, mk=None, mv=None):
    m, d = x.shape
    tm = _tile(seq_len if not decode else m, 512)
    tps = 1 if decode else seq_len // tm
    nb = m // (tm * tps)
    grid = (m // tm,)
    row = lambda w: pl.BlockSpec((tm, w), lambda i: (i, 0))
    consts = [lw['pre_g'], lw['w_in'], lw['conv_w'], lw['q_norm_g'], lw['w_uq'], lw['kv_norm_g'],
              lw['w_ukT'], lw['inv_full']]
    in_specs = [row(d)] + [_const_spec(c.shape) for c in consts]
    qcat_shape = (nb, N_HEADS, tm * tps, QK_DIM)
    qcat_spec = pl.BlockSpec((1, N_HEADS, tm, QK_DIM), lambda i: (i // tps, 0, i % tps, 0))
    sds = jax.ShapeDtypeStruct
    common_out = [
        (sds((m, RWKV_SHIFT_W), F32), row(RWKV_SHIFT_W)),
        (sds((m, GROUP_W), F32), row(GROUP_W)),
        (sds((m, KV_RANK), F32), row(KV_RANK)),
        (sds((m, ROPE_DIM), F32), row(ROPE_DIM)),
        (sds((m, QK_DIM), BF16), row(QK_DIM)),
        (sds(qcat_shape, BF16), qcat_spec),
        (sds((m, GROUP_W), F32), row(GROUP_W)),
    ]
    if decode:
        ins = [x] + consts + [conv_state[:, 0], conv_state[:, 1]]
        in_specs += [row(GROUP_W), row(GROUP_W)]
        outs = ([(sds((m, GROUP_W), BF16), row(GROUP_W)),
                 (sds((m, GROUP_W), F32), row(GROUP_W)), (sds((m, GROUP_W), F32), row(GROUP_W))]
                + common_out
                + [(sds((m, GROUP_W), F32), row(GROUP_W)), (sds((m, GROUP_W), F32), row(GROUP_W))])
        scratch = []
    else:
        ins = [x] + consts + [mk, mv]
        n_mem = mk.shape[1]
        in_specs += [pl.BlockSpec((1, n_mem, GROUP_W), lambda i: (i // tps, 0, 0))] * 2
        outs = ([(sds((m, GROUP_W), BF16), row(GROUP_W)),
                 (sds((nb, CONV_K - 1, GROUP_W), F32),
                  pl.BlockSpec((1, CONV_K - 1, GROUP_W), lambda i: (i // tps, 0, 0)))]
                + common_out
                + [(sds((m, GROUP_W), BF16), row(GROUP_W))])
        scratch = [pltpu.VMEM((8, GROUP_W), F32)]
    return pl.pallas_call(
        functools.partial(_proj_body, decode, tps, pos0),
        grid=grid, in_specs=in_specs,
        out_specs=[o[1] for o in outs], out_shape=[o[0] for o in outs],
        scratch_shapes=scratch, compiler_params=_params(("arbitrary",)),
        name="proj_decode" if decode else "proj_prompt",
    )(*ins)


def _memkv_body(x_ref, g_ref, w_ref, k_ref, v_ref, kb_ref, vb_ref):
    kv = _mm(_rms(x_ref[...], g_ref[...]), w_ref[...])
    k, v = kv[:, :GROUP_W], kv[:, GROUP_W:]
    k_ref[...] = k
    v_ref[...] = v
    kb_ref[...] = k.astype(BF16)
    vb_ref[...] = v.astype(BF16)


def _memkv_call(mem, g, w_kv):
    m, d = mem.shape
    tm = _tile(m, 512)
    row = lambda w: pl.BlockSpec((tm, w), lambda i: (i, 0))
    sds = jax.ShapeDtypeStruct
    return pl.pallas_call(
        _memkv_body, grid=(m // tm,),
        in_specs=[row(d), _const_spec(g.shape), _const_spec(w_kv.shape)],
        out_specs=[row(GROUP_W)] * 4,
        out_shape=[sds((m, GROUP_W), F32), sds((m, GROUP_W), F32), sds((m, GROUP_W), BF16), sds((m, GROUP_W), BF16)],
        compiler_params=_params(("arbitrary",)), name="mem_kv",
    )(mem, g, w_kv)


def _head_ones():
    r = lax.broadcasted_iota(jnp.int32, (GROUP_W, GROUP_W), 0) // HEAD_DIM
    c = lax.broadcasted_iota(jnp.int32, (GROUP_W, GROUP_W), 1) // HEAD_DIM
    return (r == c).astype(F32)


def _rwkv_tokens(mixed, prm, ones_bd):
    w0, a0, k_k, k_a, r_k = (prm[i:i + 1, :] for i in range(5))
    w2p, a2p = prm[8:8 + LANES, :], prm[8 + LANES:8 + 2 * LANES, :]
    r, k, v = mixed[:, 0:GROUP_W], mixed[:, GROUP_W:2 * GROUP_W], mixed[:, 2 * GROUP_W:3 * GROUP_W]
    wa = mixed[:, 3 * GROUP_W:RWKV_SHIFT_W]
    logw = -DECAY_SCALE * _sigmoid(w0 + _mm_hi(jnp.tanh(wa), w2p))
    a = _sigmoid(a0 + _mm_hi(wa, a2p))
    kkr = k * k_k
    kk = kkr * lax.rsqrt(_mm_hi(kkr * kkr, ones_bd) + 1e-12)
    k2 = k * (1.0 + (a - 1.0) * k_a)
    bonus = _mm_hi(r * k2 * r_k, ones_bd) * v
    return r, k2, v, logw, kk, a, bonus


def _rwkv_finish(y, bonus, gate, prm, ones_bd):
    lnx_g, lnx_b = prm[5:6, :], prm[6:7, :]
    mean = _mm_hi(y, ones_bd) * (1.0 / HEAD_DIM)
    yc = y - mean
    var = _mm_hi(yc * yc, ones_bd) * (1.0 / HEAD_DIM)
    return ((yc * lax.rsqrt(var + LNX_EPS)) * lnx_g + lnx_b + bonus) * gate


def _stack_heads(x, head_lane):
    return jnp.concatenate([jnp.where(head_lane == hh, x, 0.0) for hh in range(N_HEADS)], axis=0)


def _unstack_heads(x):
    c = x.shape[0] // N_HEADS
    return x[0:c] + x[c:2 * c] + x[2 * c:3 * c] + x[3 * c:4 * c]


def _rwkv_prompt_body(x_ref, gate_ref, mu_ref, prm_ref, o_ref, st_ref, prev_ref, s_ref):
    ct = x_ref.shape[1]
    si = pl.program_id(1)

    @pl.when(si == 0)
    def _():
        prev_ref[...] = jnp.zeros_like(prev_ref)
        s_ref[...] = jnp.zeros_like(s_ref)

    x = x_ref[0]
    row = lax.broadcasted_iota(jnp.int32, x.shape, 0)
    prev = jnp.where(row == 0, prev_ref[0:1, :], pltpu.roll(x, 1, 0))
    prev_ref[0:1, :] = x[ct - 1:ct, :]
    mixed = x + (prev - x) * mu_ref[...]
    prm = prm_ref[...]
    ones_bd = _head_ones()
    r, k2, v, logw, kk, a, bonus = _rwkv_tokens(mixed, prm, ones_bd)
    b = kk * a

    n = N_HEADS * CHUNK
    head_lane = lax.broadcasted_iota(jnp.int32, (CHUNK, GROUP_W), 1) // HEAD_DIM
    rr = lax.broadcasted_iota(jnp.int32, (n, n), 0)
    cc = lax.broadcasted_iota(jnp.int32, (n, n), 1)
    strict, incl = rr > cc, rr >= cc
    same_sub = (rr // SUB) == (cc // SUB)
    eye = (rr == cc).astype(F32)
    tri = (lax.broadcasted_iota(jnp.int32, (CHUNK, CHUNK), 0)
           >= lax.broadcasted_iota(jnp.int32, (CHUNK, CHUNK), 1)).astype(F32)

    s_bd = s_ref[...]
    ys = []
    for c in range(ct // CHUNK):
        sl = slice(c * CHUNK, (c + 1) * CHUNK)
        lw = logw[sl]
        cw = _mm_hi(tri, lw)
        cwl = cw[CHUNK - 1:CHUNK, :]
        e_in, e_ex, e_ng, e_w = jnp.exp(cw), jnp.exp(cw - lw), jnp.exp(-cw), jnp.exp(cwl - cw)
        v_c = v[sl]
        kkd = _stack_heads(kk[sl] * e_ex, head_lane)
        rd = _stack_heads(r[sl] * e_in, head_lane)
        kinv = _stack_heads(k2[sl] * e_ng, head_lane)
        binv = _stack_heads(b[sl] * e_ng, head_lane)
        kinvw = _stack_heads(k2[sl] * e_w, head_lane)
        binvw = _stack_heads(b[sl] * e_w, head_lane)
        v_st = _stack_heads(v_c, head_lane)
        a_bk = jnp.where(strict, _mm_nt(kkd, binv), 0.0)
        a_vk = jnp.where(strict, _mm_nt(kkd, kinv), 0.0)
        m_rb = jnp.where(incl, _mm_nt(rd, binv), 0.0)
        m_rk = jnp.where(incl, _mm_nt(rd, kinv), 0.0)
        d1 = jnp.where(same_sub, a_bk, 0.0)
        e1 = a_bk - d1
        d2 = _mm(d1, d1)
        d4 = _mm(d2, d2)
        d8 = _mm(d4, d4)
        dinv = _mm(_mm(_mm(eye - d1, eye + d2), eye + d4), eye + d8)
        nn = _mm(dinv, e1)
        tinv = _mm(_mm(eye - nn, eye + _mm(nn, nn)), dinv)
        x_st = _mm(a_vk, v_st)
        kkdp = _mm(tinv, kkd)
        vp = _mm(tinv, x_st)
        rq = rd - _mm(m_rb, kkdp)
        y_in = _mm(m_rk, v_st) - _mm(m_rb, vp)
        y = _mm_nt(_unstack_heads(rq), s_bd) + _unstack_heads(y_in)
        ys.append(y)
        w_c = jnp.exp(cwl)
        phi = eye * w_c - _mm_tn(kkdp, binvw)
        g = _mm_tn(v_st, kinvw) - _mm_tn(vp, binvw)
        s_bd = _mm(s_bd, phi) + g
    s_ref[...] = s_bd
    st_ref[0] = _unstack_heads(s_bd)
    y = jnp.concatenate(ys, axis=0) if len(ys) > 1 else ys[0]
    o_ref[...] = _rwkv_finish(y, bonus, gate_ref[...], prm, ones_bd).astype(o_ref.dtype)


def _rwkv_prompt_call(r_sh, gate, lw, nb, seq_len):
    ct = _tile(seq_len, 256)
    ns = seq_len // ct
    x3 = r_sh.reshape(nb, seq_len, RWKV_SHIFT_W)
    sds = jax.ShapeDtypeStruct
    out, st = pl.pallas_call(
        _rwkv_prompt_body, grid=(nb, ns),
        in_specs=[pl.BlockSpec((1, ct, RWKV_SHIFT_W), lambda b, s: (b, s, 0)),
                  pl.BlockSpec((ct, GROUP_W), lambda b, s: (b * ns + s, 0)),
                  _const_spec(lw['mu'].shape), _const_spec(lw['rwkv_prm'].shape)],
        out_specs=[pl.BlockSpec((ct, GROUP_W), lambda b, s: (b * ns + s, 0)),
                   pl.BlockSpec((1, HEAD_DIM, GROUP_W), lambda b, s: (b, 0, 0))],
        out_shape=[sds((nb * seq_len, GROUP_W), BF16), sds((nb, HEAD_DIM, GROUP_W), F32)],
        scratch_shapes=[pltpu.VMEM((8, RWKV_SHIFT_W), F32),
                        pltpu.VMEM((N_HEADS * HEAD_DIM, GROUP_W), F32)],
        compiler_params=_params(("arbitrary", "arbitrary")), name="rwkv_prompt",
    )(x3, gate, lw['mu'], lw['rwkv_prm'])
    st = st.reshape(nb, HEAD_DIM, N_HEADS, HEAD_DIM).transpose(0, 2, 1, 3)
    return out, st


def _rwkv_step_body(x_ref, prev_ref, gate_ref, mu_ref, prm_ref, s_ref, o_ref, so_ref, y_ref):
    tb = x_ref.shape[0]
    x = x_ref[...]
    mixed = x + (prev_ref[...] - x) * mu_ref[...]
    prm = prm_ref[...]
    ones_bd = _head_ones()
    r, k2, v, logw, kk, a, bonus = _rwkv_tokens(mixed, prm, ones_bd)
    w = jnp.exp(logw)
    b = kk * a
    eye = (lax.broadcasted_iota(jnp.int32, (HEAD_DIM, HEAD_DIM), 0)
           == lax.broadcasted_iota(jnp.int32, (HEAD_DIM, HEAD_DIM), 1))
    for bi in range(tb):
        for hh in range(N_HEADS):
            hs = slice(hh * HEAD_DIM, (hh + 1) * HEAD_DIM)
            rowv = lambda z: z[bi:bi + 1, hs]
            st = s_ref[bi, hh]
            sa = jnp.sum(st * rowv(kk), axis=1, keepdims=True)
            v_col = jnp.sum(jnp.where(eye, rowv(v), 0.0), axis=1, keepdims=True)
            st = st * rowv(w) - sa * rowv(b) + v_col * rowv(k2)
            so_ref[bi, hh] = st
            y_col = jnp.sum(st * rowv(r), axis=1, keepdims=True)
            y_ref[bi:bi + 1, hs] = jnp.sum(jnp.where(eye, y_col, 0.0), axis=0, keepdims=True)
    o_ref[...] = _rwkv_finish(y_ref[...], bonus, gate_ref[...], prm, ones_bd).astype(o_ref.dtype)


def _rwkv_step_call(r_sh, prev, gate, state, lw):
    m = r_sh.shape[0]
    tb = _tile(m, 8)
    row = lambda w: pl.BlockSpec((tb, w), lambda i: (i, 0))
    st_spec = pl.BlockSpec((tb, N_HEADS, HEAD_DIM, HEAD_DIM), lambda i: (i, 0, 0, 0))
    sds = jax.ShapeDtypeStruct
    return pl.pallas_call(
        _rwkv_step_body, grid=(m // tb,),
        in_specs=[row(RWKV_SHIFT_W), row(RWKV_SHIFT_W), row(GROUP_W),
                  _const_spec(lw['mu'].shape), _const_spec(lw['rwkv_prm'].shape), st_spec],
        out_specs=[row(GROUP_W), st_spec],
        out_shape=[sds((m, GROUP_W), BF16), sds(state.shape, F32)],
        scratch_shapes=[pltpu.VMEM((tb, GROUP_W), F32)],
        compiler_params=_params(("arbitrary",)), name="rwkv_step",
    )(r_sh, prev, gate, lw['mu'], lw['rwkv_prm'], state)


def _softmax_update(s, v, m_sc, l_sc, acc_sc):
    m_prev = m_sc[...]
    m_new = jnp.maximum(m_prev, jnp.max(s, axis=-1, keepdims=True))
    alpha = jnp.exp(m_prev - m_new)
    p = jnp.exp(s - m_new)
    l_sc[...] = alpha * l_sc[...] + jnp.sum(p, axis=-1, keepdims=True)
    acc_sc[...] = alpha * acc_sc[...] + _mm(p, v)
    m_sc[...] = m_new


def _flash_body(qi_ref, ki_ref, q_ref, k_ref, gate_ref, wuv_ref, o_ref, m_sc, l_sc, acc_sc):
    p = pl.program_id(1)
    qi, ki = qi_ref[p], ki_ref[p]
    tq = q_ref.shape[2]

    @pl.when(ki == 0)
    def _():
        m_sc[...] = jnp.full_like(m_sc, NEG_INF)
        l_sc[...] = jnp.zeros_like(l_sc)
        acc_sc[...] = jnp.zeros_like(acc_sc)

    q = q_ref[0].reshape(N_HEADS * tq, QK_DIM)
    k = k_ref[0]
    s = _mm_nt(q, k)
    v = k[:, 0:KV_RANK]

    @pl.when(ki < qi)
    def _():
        _softmax_update(s, v, m_sc, l_sc, acc_sc)

    @pl.when(ki == qi)
    def _():
        qpos = lax.broadcasted_iota(jnp.int32, s.shape, 0) % tq
        kpos = lax.broadcasted_iota(jnp.int32, s.shape, 1)
        _softmax_update(jnp.where(kpos <= qpos, s, NEG_INF), v, m_sc, l_sc, acc_sc)
        o = acc_sc[...] / l_sc[...]
        out = jnp.zeros((tq, GROUP_W), F32)
        for hh in range(N_HEADS):
            out = out + _mm(o[hh * tq:(hh + 1) * tq], wuv_ref[hh])
        o_ref[...] = (out * gate_ref[...]).astype(o_ref.dtype)


def _flash_call(qcat, kcat, gate, w_uv_pad, seq_len):
    nb = qcat.shape[0]
    tq = _tile(seq_len, 256)
    nq = seq_len // tq
    pairs = [(qi, ki) for qi in range(nq) for ki in range(qi + 1)]
    qi_tab = jnp.asarray(np.array([p[0] for p in pairs], np.int32))
    ki_tab = jnp.asarray(np.array([p[1] for p in pairs], np.int32))
    k3 = kcat.reshape(nb, seq_len, QK_DIM)
    grid_spec = pltpu.PrefetchScalarGridSpec(
        num_scalar_prefetch=2, grid=(nb, len(pairs)),
        in_specs=[pl.BlockSpec((1, N_HEADS, tq, QK_DIM), lambda b, p, qt, kt: (b, 0, qt[p], 0)),
                  pl.BlockSpec((1, tq, QK_DIM), lambda b, p, qt, kt: (b, kt[p], 0)),
                  pl.BlockSpec((tq, GROUP_W), lambda b, p, qt, kt: (b * nq + qt[p], 0)),
                  pl.BlockSpec(w_uv_pad.shape, lambda b, p, qt, kt: (0, 0, 0))],
        out_specs=pl.BlockSpec((tq, GROUP_W), lambda b, p, qt, kt: (b * nq + qt[p], 0)),
        scratch_shapes=[pltpu.VMEM((N_HEADS * tq, 1), F32), pltpu.VMEM((N_HEADS * tq, 1), F32),
                        pltpu.VMEM((N_HEADS * tq, KV_RANK), F32)])
    return pl.pallas_call(
        _flash_body, grid_spec=grid_spec,
        out_shape=jax.ShapeDtypeStruct((nb * seq_len, GROUP_W), BF16),
        compiler_params=_params(("arbitrary", "arbitrary")), name="flash_mla",
    )(qi_tab, ki_tab, qcat, k3, gate, w_uv_pad)


def _decode_body(layer, nch, pg, pt_ref, q_ref, cnew_ref, knew_ref, gate_ref, wuv_ref, ckv_hbm, kr_hbm,
                 o_ref, cbuf, kbuf, sem, m_sc, l_sc, acc_sc):
    b, c = pl.program_id(0), pl.program_id(1)
    step = b * nch + c
    total = pl.num_programs(0) * nch
    slot = step % 2

    def page_copies(st, sl):
        bb, cc = st // nch, st % nch
        out = []
        for p in range(pg):
            page = pt_ref[bb, cc * pg + p]
            out.append(pltpu.make_async_copy(ckv_hbm.at[layer, page], cbuf.at[sl, p], sem.at[0, sl]))
            out.append(pltpu.make_async_copy(kr_hbm.at[layer, page], kbuf.at[sl, p], sem.at[1, sl]))
        return out

    @pl.when(step == 0)
    def _():
        for cp in page_copies(0, 0):
            cp.start()

    @pl.when(step + 1 < total)
    def _():
        for cp in page_copies(step + 1, 1 - slot):
            cp.start()

    @pl.when(c == 0)
    def _():
        m_sc[...] = jnp.full_like(m_sc, NEG_INF)
        l_sc[...] = jnp.zeros_like(l_sc)
        acc_sc[...] = jnp.zeros_like(acc_sc)

    for cp in page_copies(step, slot):
        cp.wait()

    page = cbuf.shape[2]
    ck = cbuf[slot].reshape(pg * page, KV_RANK).astype(BF16)
    kr = kbuf[slot].reshape(pg * page, ROPE_DIM).astype(BF16)
    q = q_ref[0]
    s = _mm_nt(q[:, 0:KV_RANK], ck) + _mm_nt(q[:, KV_RANK:QK_DIM], kr)
    _softmax_update(s, ck, m_sc, l_sc, acc_sc)

    @pl.when(c == nch - 1)
    def _():
        qf = q.astype(F32)
        cn, kn = cnew_ref[0], knew_ref[0]
        s_new = (jnp.sum(qf[:, 0:KV_RANK] * cn, axis=-1, keepdims=True)
                 + jnp.sum(qf[:, KV_RANK:QK_DIM] * kn, axis=-1, keepdims=True))
        m_prev = m_sc[...]
        m_new = jnp.maximum(m_prev, s_new)
        alpha, p_new = jnp.exp(m_prev - m_new), jnp.exp(s_new - m_new)
        l_fin = alpha * l_sc[...] + p_new
        o = (alpha * acc_sc[...] + p_new * cn) / l_fin
        out = jnp.zeros((1, GROUP_W), F32)
        for hh in range(N_HEADS):
            out = out + _mm(o, wuv_ref[hh])[hh:hh + 1, :]
        o_ref[0] = out * gate_ref[0]


def _decode_call(layer, page_table, q16, ckv_new, kr_new, gate, w_uv_pad, cache_ckv, cache_krope):
    nb, n_pages = page_table.shape
    page = cache_ckv.shape[2]
    pg = _tile(n_pages, 32)
    nch = n_pages // pg
    blk = lambda shape: pl.BlockSpec((1,) + shape, lambda b, c, pt: (b, 0, 0))
    grid_spec = pltpu.PrefetchScalarGridSpec(
        num_scalar_prefetch=1, grid=(nb, nch),
        in_specs=[blk((16, QK_DIM)), blk((1, KV_RANK)), blk((1, ROPE_DIM)), blk((1, GROUP_W)),
                  pl.BlockSpec(w_uv_pad.shape, lambda b, c, pt: (0, 0, 0)),
                  pl.BlockSpec(memory_space=pl.ANY), pl.BlockSpec(memory_space=pl.ANY)],
        out_specs=blk((1, GROUP_W)),
        scratch_shapes=[pltpu.VMEM((2, pg, page, KV_RANK), F32), pltpu.VMEM((2, pg, page, ROPE_DIM), F32),
                        pltpu.SemaphoreType.DMA((2, 2)),
                        pltpu.VMEM((16, 1), F32), pltpu.VMEM((16, 1), F32), pltpu.VMEM((16, KV_RANK), F32)])
    out = pl.pallas_call(
        functools.partial(_decode_body, layer, nch, pg), grid_spec=grid_spec,
        out_shape=jax.ShapeDtypeStruct((nb, 1, GROUP_W), F32),
        compiler_params=_params(("arbitrary", "arbitrary")), name="paged_decode",
    )(page_table, q16, ckv_new[:, None, :], kr_new[:, None, :], gate[:, None, :], w_uv_pad,
      cache_ckv, cache_krope)
    return out.reshape(nb, GROUP_W)


def _memattn_body(q_ref, gate_ref, k_ref, v_ref, o_ref):
    tb = q_ref.shape[0]
    head = lax.broadcasted_iota(jnp.int32, (8, GROUP_W), 1) // HEAD_DIM
    sel = head == lax.broadcasted_iota(jnp.int32, (8, GROUP_W), 0)
    for bi in range(tb):
        q_bd = jnp.where(sel, q_ref[bi], 0.0)
        s = _mm_nt(q_bd, k_ref[0, bi])
        p = jnp.exp(s - jnp.max(s, axis=-1, keepdims=True))
        p = p / jnp.sum(p, axis=-1, keepdims=True)
        o = jnp.where(sel, _mm(p, v_ref[0, bi]), 0.0)
        o_ref[bi] = jnp.sum(o, axis=0, keepdims=True) * gate_ref[bi]


def _memattn_call(layer, mem_q, gate, mem_k, mem_v):
    m = mem_q.shape[0]
    n_mem = mem_k.shape[2]
    tb = _tile(m, 4)
    vec = pl.BlockSpec((tb, 1, GROUP_W), lambda i: (i, 0, 0))
    kv = pl.BlockSpec((1, tb, n_mem, GROUP_W), lambda i: (layer, i, 0, 0))
    out = pl.pallas_call(
        _memattn_body, grid=(m // tb,),
        in_specs=[vec, vec, kv, kv], out_specs=vec,
        out_shape=jax.ShapeDtypeStruct((m, 1, GROUP_W), F32),
        compiler_params=_params(("arbitrary",)), name="mem_attn_decode",
    )(mem_q[:, None, :], gate[:, None, :], mem_k, mem_v)
    return out.reshape(m, GROUP_W)


def _outproj_body(x_ref, m0_ref, m1_ref, m2_ref, m3_ref, w_ref, g_ref, o_ref):
    acc = jnp.zeros(x_ref.shape, F32)
    for i, m_ref in enumerate((m0_ref, m1_ref, m2_ref, m3_ref)):
        acc = acc + _mm(m_ref[...], w_ref[i * GROUP_W:(i + 1) * GROUP_W, :])
    o_ref[...] = x_ref[...] + _rms(acc, g_ref[...])


def _outproj_call(x, mixed, w_out, g):
    m, d = x.shape
    tm = _tile(m, 512)
    row = lambda w: pl.BlockSpec((tm, w), lambda i: (i, 0))
    return pl.pallas_call(
        _outproj_body, grid=(m // tm,),
        in_specs=[row(d)] + [row(GROUP_W)] * 4 + [_const_spec(w_out.shape), _const_spec(g.shape)],
        out_specs=row(d), out_shape=jax.ShapeDtypeStruct((m, d), F32),
        compiler_params=_params(("arbitrary",)), name="out_proj",
    )(x, *mixed, w_out, g)


def _prep_layer(l, pre_norm_g, post_norm_g, w_in, w_out, conv_w, rwkv_mu, rwkv_w0, rwkv_w2, rwkv_a0, rwkv_a2,
                rwkv_k_k, rwkv_k_a, rwkv_r_k, rwkv_lnx_g, rwkv_lnx_b, mla_q_norm_g, mla_w_uq, mla_kv_norm_g,
                mla_w_uk, mla_w_uv, mem_norm_g, w_mem_k, w_mem_v):
    d = w_in.shape[1]
    kr0 = _C_MLAG
    w = w_in[l]
    w_in_p = jnp.concatenate([w[:, :kr0], w[:, kr0 + ROPE_DIM:], w[:, kr0:kr0 + ROPE_DIM],
                              jnp.zeros((d, LANES - ROPE_DIM), F32)], axis=1).astype(BF16)
    uq = mla_w_uq[l].reshape(Q_RANK, N_HEADS, NOPE_DIM + ROPE_DIM)
    w_uq = jnp.concatenate([uq[:, :, :NOPE_DIM].reshape(Q_RANK, -1), uq[:, :, NOPE_DIM:].reshape(Q_RANK, -1)],
                           axis=1).astype(BF16)
    w_uv_pad = jnp.zeros((N_HEADS, KV_RANK, GROUP_W), F32)
    for hh in range(N_HEADS):
        w_uv_pad = w_uv_pad.at[hh, :, hh * HEAD_DIM:(hh + 1) * HEAD_DIM].set(mla_w_uv[l][:, hh, :])
    half = ROPE_DIM // 2
    inv = jnp.power(ROPE_BASE, -jnp.arange(half, dtype=F32) / half)
    zl = jnp.zeros((LORA_W, GROUP_W), F32)
    vecs = jnp.stack([rwkv_w0[l], rwkv_a0[l], rwkv_k_k[l], rwkv_k_a[l], rwkv_r_k[l].reshape(-1),
                      rwkv_lnx_g[l], rwkv_lnx_b[l], jnp.zeros((GROUP_W,), F32)])
    return {
        'pre_g': pre_norm_g[l][None], 'post_g': post_norm_g[l][None], 'w_in': w_in_p,
        'w_out': w_out[l].astype(BF16), 'conv_w': conv_w[l], 'q_norm_g': mla_q_norm_g[l][None], 'w_uq': w_uq,
        'kv_norm_g': mla_kv_norm_g[l][None], 'w_ukT': jnp.transpose(mla_w_uk[l], (1, 2, 0)).astype(BF16),
        'w_uv_pad': w_uv_pad.astype(BF16), 'inv_full': jnp.tile(inv, LANES // half)[None],
        'mu': rwkv_mu[l][None],
        'rwkv_prm': jnp.concatenate([vecs, rwkv_w2[l], zl, zl, rwkv_a2[l]], axis=0),
        'mem_g': mem_norm_g[l][None],
        'w_mem_kv': jnp.concatenate([w_mem_k[l], w_mem_v[l]], axis=1).astype(BF16),
    }


def kernel(x_prompt, x_sample, cache_ckv, cache_krope, cache_mem_k, cache_mem_v, state_conv, state_rwkv_shift, state_rwkv, page_table, mem_prompt, pre_norm_g, post_norm_g, w_in, w_out, conv_w, rwkv_mu, rwkv_w0, rwkv_w2, rwkv_a0, rwkv_a2, rwkv_k_k, rwkv_k_a, rwkv_r_k, rwkv_lnx_g, rwkv_lnx_b, mla_q_norm_g, mla_w_uq, mla_kv_norm_g, mla_w_uk, mla_w_uv, mem_norm_g, w_mem_k, w_mem_v):
    bp, sp, d = x_prompt.shape
    bs, ts, _ = x_sample.shape
    assert ts == 1
    depth = w_in.shape[0]
    n_mem = mem_prompt.shape[1]
    past_len = page_table.shape[1] * cache_ckv.shape[2]
    mem_k4 = cache_mem_k.reshape(depth, bs, n_mem, GROUP_W)
    mem_v4 = cache_mem_v.reshape(depth, bs, n_mem, GROUP_W)
    xp = x_prompt.reshape(bp * sp, d)
    xs = x_sample.reshape(bs, d)
    mem2 = mem_prompt.reshape(bp * n_mem, d)
    outs = [[] for _ in range(12)]
    for l in range(depth):
        lw = _prep_layer(l, pre_norm_g, post_norm_g, w_in, w_out, conv_w, rwkv_mu, rwkv_w0, rwkv_w2, rwkv_a0,
                         rwkv_a2, rwkv_k_k, rwkv_k_a, rwkv_r_k, rwkv_lnx_g, rwkv_lnx_b, mla_q_norm_g, mla_w_uq,
                         mla_kv_norm_g, mla_w_uk, mla_w_uv, mem_norm_g, w_mem_k, w_mem_v)
        mk, mv, mk_b, mv_b = _memkv_call(mem2, lw['mem_g'], lw['w_mem_kv'])
        (m_conv, conv_p, r_sh, r_gate, ckv, kr, kcat, qcat, mla_gate, m_mem) = _proj_call(
            xp, lw, decode=False, seq_len=sp,
            mk=mk_b.reshape(bp, n_mem, GROUP_W), mv=mv_b.reshape(bp, n_mem, GROUP_W))
        m_rwkv, st_p = _rwkv_prompt_call(r_sh, r_gate, lw, bp, sp)
        m_mla = _flash_call(qcat, kcat, mla_gate, lw['w_uv_pad'], sp)
        xp = _outproj_call(xp, (m_conv, m_rwkv, m_mla, m_mem), lw['w_out'], lw['post_g'])
        sh_p = r_sh.reshape(bp, sp, RWKV_SHIFT_W)[:, -1]
        (s_conv, cn0, cn1, r_sh_s, r_gate_s, ckv_s, kr_s, _, qcat_s, mla_gate_s, mem_q_s, mem_gate_s) = _proj_call(
            xs, lw, decode=True, seq_len=1, pos0=float(past_len), conv_state=state_conv[l])
        s_rwkv, st_s = _rwkv_step_call(r_sh_s, state_rwkv_shift[l], r_gate_s, state_rwkv[l], lw)
        q16 = jnp.pad(jnp.transpose(qcat_s[0], (1, 0, 2)), ((0, 0), (0, 16 - N_HEADS), (0, 0)))
        s_mla = _decode_call(l, page_table, q16, ckv_s, kr_s, mla_gate_s, lw['w_uv_pad'], cache_ckv, cache_krope)
        s_mem = _memattn_call(l, mem_q_s, mem_gate_s, mem_k4, mem_v4)
        xs = _outproj_call(xs, (s_conv, s_rwkv, s_mla, s_mem), lw['w_out'], lw['post_g'])
        vals = (ckv.reshape(bp, sp, KV_RANK), ckv_s.reshape(bs, ts, KV_RANK),
                kr.reshape(bp, sp, ROPE_DIM), kr_s.reshape(bs, ts, ROPE_DIM),
                mk.reshape(bp, n_mem, N_HEADS, HEAD_DIM), mv.reshape(bp, n_mem, N_HEADS, HEAD_DIM),
                conv_p, jnp.stack([cn0, cn1], axis=1), sh_p, r_sh_s, st_p, st_s)
        for o, v in zip(outs, vals):
            o.append(v)
    return (xp.reshape(bp, sp, d), xs.reshape(bs, ts, d)) + tuple(jnp.stack(o) for o in outs)
```

```python
import functools

import jax
import jax.numpy as jnp
import numpy as np
from jax import lax
from jax.experimental import pallas as pl
from jax.experimental.pallas import tpu as pltpu

F32 = jnp.float32
BF16 = jnp.bfloat16
HIGHEST = lax.Precision.HIGHEST

HEAD_DIM = 64
N_HEADS = 4
GROUP_W = N_HEADS * HEAD_DIM
CONV_K = 3
LORA_W = 64
RWKV_SHIFT_W = 3 * GROUP_W + 2 * LORA_W
Q_RANK = 256
KV_RANK = 128
NOPE_DIM = 64
ROPE_DIM = 32
QK_DIM = KV_RANK + ROPE_DIM
QK_W = 2 * KV_RANK
ONES_LANE = QK_DIM
RMS_EPS = 1e-6
LNX_EPS = 64e-5
ROPE_BASE = 10000.0
DECAY_SCALE = 0.6065306597
MLA_SCALE = (NOPE_DIM + ROPE_DIM) ** -0.5
MEM_SCALE = HEAD_DIM ** -0.5
NEG_INF = -1e30

LANES = 128
CHUNK = 64
SUB = 16
VMEM_LIMIT = 56 * 1024 * 1024

_C_CONV, _C_RSH, _C_RGATE, _C_QDOWN, _C_CKV, _C_MLAG, _C_MEMQ, _C_MEMG, _C_KR, _C_END = (
    0, 1024, 1920, 2176, 2432, 2560, 2816, 3072, 3328, 3456)


def _mm(a, b):
    return jnp.dot(a.astype(BF16), b.astype(BF16), preferred_element_type=F32)


def _mm_nt(a, b):
    return lax.dot_general(a.astype(BF16), b.astype(BF16), (((1,), (1,)), ((), ())),
                           preferred_element_type=F32)


def _mm_tn(a, b):
    return lax.dot_general(a.astype(BF16), b.astype(BF16), (((0,), (0,)), ((), ())),
                           preferred_element_type=F32)


def _split_bf16(x, n):
    parts = []
    for _ in range(n):
        p = x.astype(BF16)
        parts.append(p)
        x = x - p.astype(F32)
    return parts


def _mm_hi(a, b):
    (a1, a2), (b1, b2) = _split_bf16(a, 2), _split_bf16(b, 2)
    dot = lambda x, y: jnp.dot(x, y, preferred_element_type=F32)
    return dot(a1, b1) + (dot(a1, b2) + dot(a2, b1))


def _mm_exact_lhs(a01, b):
    a = a01.astype(BF16)
    b1, b2, b3 = _split_bf16(b, 3)
    dot = lambda x: jnp.dot(a, x, preferred_element_type=F32)
    return dot(b1) + (dot(b2) + dot(b3))


def _sigmoid(x):
    return 1.0 / (1.0 + jnp.exp(-x))


def _silu(x):
    return x * _sigmoid(x)


def _rms(x, g):
    return x * lax.rsqrt(jnp.mean(x * x, axis=-1, keepdims=True) + RMS_EPS) * g


def _params(sem):
    return pltpu.CompilerParams(dimension_semantics=sem, vmem_limit_bytes=VMEM_LIMIT)


def _const_spec(shape):
    nd = len(shape)
    return pl.BlockSpec(shape, lambda *_: (0,) * nd)


def _tile(n, pref):
    t = min(n, pref)
    assert n % t == 0, (n, pref)
    return t


def _proj_body(decode, tiles_per_seq, pos0, *refs):
    if decode:
        (x_ref, g_ref, w_ref, cw_ref, qg_ref, wuq_ref, kvg_ref, wuk_ref, inv_ref, cb0_ref, cb1_ref,
         mconv_ref, cn0_ref, cn1_ref, rsh_ref, rgate_ref, ckv_ref, kr_ref, kcat_ref, qcat_ref,
         mlag_ref, memq_ref, memg_ref) = refs
    else:
        (x_ref, g_ref, w_ref, cw_ref, qg_ref, wuq_ref, kvg_ref, wuk_ref, inv_ref, mk_ref, mv_ref,
         mconv_ref, cst_ref, rsh_ref, rgate_ref, ckv_ref, kr_ref, kcat_ref, qcat_ref,
         mlag_ref, mmem_ref, carry_ref) = refs
    tm = x_ref.shape[0]
    t = pl.program_id(0) % tiles_per_seq
    h = _rms(x_ref[...], g_ref[...]).astype(BF16)

    def proj(lo, hi):
        return jnp.dot(h, w_ref[:, lo:hi], preferred_element_type=F32)

    cv = proj(_C_CONV, _C_RSH)
    c_b, c_c, c_x, c_g = (cv[:, i * GROUP_W:(i + 1) * GROUP_W] for i in range(4))
    u = c_c * c_x
    w0, w1, w2 = cw_ref[0:1, :], cw_ref[1:2, :], cw_ref[2:3, :]
    if decode:
        b0, b1 = cb0_ref[...], cb1_ref[...]
        y = b0 * w0 + b1 * w1 + u * w2
        cn0_ref[...] = b1
        cn1_ref[...] = u
    else:
        @pl.when(t == 0)
        def _():
            carry_ref[...] = jnp.zeros_like(carry_ref)
        p0, p1 = carry_ref[0:1, :], carry_ref[1:2, :]
        row = lax.broadcasted_iota(jnp.int32, u.shape, 0)
        u1 = jnp.where(row == 0, p1, pltpu.roll(u, 1, 0))
        u2 = jnp.where(row == 0, p0, jnp.where(row == 1, p1, pltpu.roll(u, 2, 0)))
        y = u2 * w0 + u1 * w1 + u * w2
        carry_ref[0:2, :] = u[tm - 2:tm, :]
        cst_ref[0] = u[tm - 2:tm, :]
    mconv_ref[...] = (c_b * y * _silu(c_g)).astype(mconv_ref.dtype)

    rsh_ref[...] = proj(_C_RSH, _C_RGATE)
    rgate_ref[...] = _silu(proj(_C_RGATE, _C_QDOWN))

    if decode:
        pos = jnp.full((tm, 1), pos0, F32)
    else:
        pos = (t * tm + lax.broadcasted_iota(jnp.int32, (tm, 1), 0)).astype(F32)
    ang = pos * inv_ref[...]
    cos, sin = jnp.cos(ang), jnp.sin(ang)
    first = (lax.broadcasted_iota(jnp.int32, (tm, LANES), 1) % ROPE_DIM) < (ROPE_DIM // 2)

    def rope(v):
        partner = jnp.where(first, -pltpu.roll(v, LANES - ROPE_DIM // 2, 1), pltpu.roll(v, ROPE_DIM // 2, 1))
        return v * cos + partner * sin

    qn = _rms(proj(_C_QDOWN, _C_CKV), qg_ref[...])
    q = _mm(qn, wuq_ref[...])
    q_rope = rope(q[:, GROUP_W:GROUP_W + LANES]) * MLA_SCALE
    lane = lax.broadcasted_iota(jnp.int32, (tm, LANES), 1)
    for hh in range(N_HEADS):
        q_lat = _mm(q[:, hh * NOPE_DIM:(hh + 1) * NOPE_DIM], wuk_ref[hh]) * MLA_SCALE
        qcat_ref[0, hh, :, 0:KV_RANK] = q_lat.astype(qcat_ref.dtype)
        q_r = q_rope if hh == 0 else pltpu.roll(q_rope, LANES - hh * ROPE_DIM, 1)
        qcat_ref[0, hh, :, KV_RANK:QK_W] = jnp.where(lane < ROPE_DIM, q_r, 0.0).astype(qcat_ref.dtype)
    ckv = _rms(proj(_C_CKV, _C_MLAG), kvg_ref[...])
    krope = rope(proj(_C_KR, _C_END))
    ckv_ref[...] = ckv
    kr_ref[...] = krope[:, 0:ROPE_DIM]
    kcat_ref[:, 0:KV_RANK] = ckv.astype(kcat_ref.dtype)
    kcat_ref[:, KV_RANK:QK_W] = jnp.where(lane == ONES_LANE - KV_RANK, 1.0, krope).astype(kcat_ref.dtype)
    mlag_ref[...] = _silu(proj(_C_MLAG, _C_MEMQ))

    mq = proj(_C_MEMQ, _C_MEMG) * MEM_SCALE
    mg = _silu(proj(_C_MEMG, _C_KR))
    if decode:
        memq_ref[...] = mq
        memg_ref[...] = mg
    else:
        mk, mv = mk_ref[0], mv_ref[0]
        head = lax.broadcasted_iota(jnp.int32, (tm, GROUP_W), 1) // HEAD_DIM
        acc = jnp.zeros((tm, GROUP_W), F32)
        for hh in range(N_HEADS):
            s = _mm_nt(jnp.where(head == hh, mq, 0.0), mk)
            p = jnp.exp(s - jnp.max(s, axis=-1, keepdims=True))
            p = p / jnp.sum(p, axis=-1, keepdims=True)
            acc = acc + jnp.where(head == hh, _mm(p, mv), 0.0)
        mmem_ref[...] = (acc * mg).astype(mmem_ref.dtype)


def _proj_call(x, lw, *, decode, seq_len, pos0=0.0, conv_state=None, mk=None, mv=None):
    m, d = x.shape
    tm = _tile(seq_len if not decode else m, 512)
    tps = 1 if decode else seq_len // tm
    nb = m // (tm * tps)
    grid = (m // tm,)
    row = lambda w: pl.BlockSpec((tm, w), lambda i: (i, 0))
    consts = [lw['pre_g'], lw['w_in'], lw['conv_w'], lw['q_norm_g'], lw['w_uq'], lw['kv_norm_g'],
              lw['w_ukT'], lw['inv_full']]
    in_specs = [row(d)] + [_const_spec(c.shape) for c in consts]
    qcat_shape = (nb, N_HEADS, tm * tps, QK_W)
    qcat_spec = pl.BlockSpec((1, N_HEADS, tm, QK_W), lambda i: (i // tps, 0, i % tps, 0))
    sds = jax.ShapeDtypeStruct
    common_out = [
        (sds((m, RWKV_SHIFT_W), F32), row(RWKV_SHIFT_W)),
        (sds((m, GROUP_W), F32), row(GROUP_W)),
        (sds((m, KV_RANK), F32), row(KV_RANK)),
        (sds((m, ROPE_DIM), F32), row(ROPE_DIM)),
        (sds((m, QK_W), BF16), row(QK_W)),
        (sds(qcat_shape, BF16), qcat_spec),
        (sds((m, GROUP_W), F32), row(GROUP_W)),
    ]
    if decode:
        ins = [x] + consts + [conv_state[:, 0], conv_state[:, 1]]
        in_specs += [row(GROUP_W), row(GROUP_W)]
        outs = ([(sds((m, GROUP_W), BF16), row(GROUP_W)),
                 (sds((m, GROUP_W), F32), row(GROUP_W)), (sds((m, GROUP_W), F32), row(GROUP_W))]
                + common_out
                + [(sds((m, GROUP_W), F32), row(GROUP_W)), (sds((m, GROUP_W), F32), row(GROUP_W))])
        scratch = []
    else:
        ins = [x] + consts + [mk, mv]
        n_mem = mk.shape[1]
        in_specs += [pl.BlockSpec((1, n_mem, GROUP_W), lambda i: (i // tps, 0, 0))] * 2
        outs = ([(sds((m, GROUP_W), BF16), row(GROUP_W)),
                 (sds((nb, CONV_K - 1, GROUP_W), F32),
                  pl.BlockSpec((1, CONV_K - 1, GROUP_W), lambda i: (i // tps, 0, 0)))]
                + common_out
                + [(sds((m, GROUP_W), BF16), row(GROUP_W))])
        scratch = [pltpu.VMEM((8, GROUP_W), F32)]
    return pl.pallas_call(
        functools.partial(_proj_body, decode, tps, pos0),
        grid=grid, in_specs=in_specs,
        out_specs=[o[1] for o in outs], out_shape=[o[0] for o in outs],
        scratch_shapes=scratch, compiler_params=_params(("arbitrary",)),
        name="proj_decode" if decode else "proj_prompt",
    )(*ins)


def _memkv_body(x_ref, g_ref, w_ref, k_ref, v_ref, kb_ref, vb_ref):
    kv = _mm(_rms(x_ref[...], g_ref[...]), w_ref[...])
    k, v = kv[:, :GROUP_W], kv[:, GROUP_W:]
    k_ref[...] = k
    v_ref[...] = v
    kb_ref[...] = k.astype(BF16)
    vb_ref[...] = v.astype(BF16)


def _memkv_call(mem, g, w_kv):
    m, d = mem.shape
    tm = _tile(m, 512)
    row = lambda w: pl.BlockSpec((tm, w), lambda i: (i, 0))
    sds = jax.ShapeDtypeStruct
    return pl.pallas_call(
        _memkv_body, grid=(m // tm,),
        in_specs=[row(d), _const_spec(g.shape), _const_spec(w_kv.shape)],
        out_specs=[row(GROUP_W)] * 4,
        out_shape=[sds((m, GROUP_W), F32), sds((m, GROUP_W), F32), sds((m, GROUP_W), BF16), sds((m, GROUP_W), BF16)],
        compiler_params=_params(("arbitrary",)), name="mem_kv",
    )(mem, g, w_kv)


def _head_ones():
    r = lax.broadcasted_iota(jnp.int32, (GROUP_W, GROUP_W), 0) // HEAD_DIM
    c = lax.broadcasted_iota(jnp.int32, (GROUP_W, GROUP_W), 1) // HEAD_DIM
    return (r == c).astype(F32)


def _rwkv_tokens(mixed, prm, ones_bd):
    w0, a0, k_k, k_a, r_k = (prm[i:i + 1, :] for i in range(5))
    w2p, a2p = prm[8:8 + LANES, :], prm[8 + LANES:8 + 2 * LANES, :]
    r, k, v = mixed[:, 0:GROUP_W], mixed[:, GROUP_W:2 * GROUP_W], mixed[:, 2 * GROUP_W:3 * GROUP_W]
    wa = mixed[:, 3 * GROUP_W:RWKV_SHIFT_W]
    logw = -DECAY_SCALE * _sigmoid(w0 + _mm_hi(jnp.tanh(wa), w2p))
    a = _sigmoid(a0 + _mm_hi(wa, a2p))
    kkr = k * k_k
    kk = kkr * lax.rsqrt(_mm(kkr * kkr, ones_bd) + 1e-12)
    k2 = k * (1.0 + (a - 1.0) * k_a)
    bonus = _mm(r * k2 * r_k, ones_bd) * v
    return r, k2, v, logw, kk, a, bonus


def _rwkv_finish(y, bonus, gate, prm, ones_bd):
    lnx_g, lnx_b = prm[5:6, :], prm[6:7, :]
    mean = _mm(y, ones_bd) * (1.0 / HEAD_DIM)
    yc = y - mean
    var = _mm(yc * yc, ones_bd) * (1.0 / HEAD_DIM)
    return ((yc * lax.rsqrt(var + LNX_EPS)) * lnx_g + lnx_b + bonus) * gate


def _stack_heads(x, head_lane):
    return jnp.concatenate([jnp.where(head_lane == hh, x, 0.0) for hh in range(N_HEADS)], axis=0)


def _unstack_heads(x):
    c = x.shape[0] // N_HEADS
    return x[0:c] + x[c:2 * c] + x[2 * c:3 * c] + x[3 * c:4 * c]


def _rwkv_prompt_body(x_ref, gate_ref, mu_ref, prm_ref, o_ref, st_ref, prev_ref, s_ref):
    ct = x_ref.shape[1]
    si = pl.program_id(1)

    @pl.when(si == 0)
    def _():
        prev_ref[...] = jnp.zeros_like(prev_ref)
        s_ref[...] = jnp.zeros_like(s_ref)

    x = x_ref[0]
    row = lax.broadcasted_iota(jnp.int32, x.shape, 0)
    prev = jnp.where(row == 0, prev_ref[0:1, :], pltpu.roll(x, 1, 0))
    prev_ref[0:1, :] = x[ct - 1:ct, :]
    mixed = x + (prev - x) * mu_ref[...]
    prm = prm_ref[...]
    ones_bd = _head_ones()
    r, k2, v, logw, kk, a, bonus = _rwkv_tokens(mixed, prm, ones_bd)
    b = kk * a

    n = N_HEADS * CHUNK
    head_lane = lax.broadcasted_iota(jnp.int32, (CHUNK, GROUP_W), 1) // HEAD_DIM
    rr = lax.broadcasted_iota(jnp.int32, (n, n), 0)
    cc = lax.broadcasted_iota(jnp.int32, (n, n), 1)
    strict, incl = rr > cc, rr >= cc
    same_sub = (rr // SUB) == (cc // SUB)
    eye = (rr == cc).astype(F32)
    tri = (lax.broadcasted_iota(jnp.int32, (CHUNK, CHUNK), 0)
           >= lax.broadcasted_iota(jnp.int32, (CHUNK, CHUNK), 1)).astype(F32)

    chunks = range(ct // CHUNK)
    each = lambda f, *xs: [f(*(x[c] for x in xs)) for c in chunks]
    rows = lambda z: [z[c * CHUNK:(c + 1) * CHUNK] for c in chunks]
    stack = lambda z: _stack_heads(z, head_lane)
    lw = rows(logw)
    cw = each(lambda l: _mm_exact_lhs(tri, l), lw)
    cwl = each(lambda z: z[CHUNK - 1:CHUNK, :], cw)
    e_ng = each(lambda z: jnp.exp(-z), cw)
    e_w = each(lambda z, zl: jnp.exp(zl - z), cw, cwl)
    kkd = each(lambda x, z, l: stack(x * jnp.exp(z - l)), rows(kk), cw, lw)
    rd = each(lambda x, z: stack(x * jnp.exp(z)), rows(r), cw)
    kinv = each(lambda x, e: stack(x * e), rows(k2), e_ng)
    binv = each(lambda x, e: stack(x * e), rows(b), e_ng)
    kinvw = each(lambda x, e: stack(x * e), rows(k2), e_w)
    binvw = each(lambda x, e: stack(x * e), rows(b), e_w)
    v_st = each(stack, rows(v))
    a_bk = each(lambda x, y: jnp.where(strict, _mm_nt(x, y), 0.0), kkd, binv)
    a_vk = each(lambda x, y: jnp.where(strict, _mm_nt(x, y), 0.0), kkd, kinv)
    m_rb = each(lambda x, y: jnp.where(incl, _mm_nt(x, y), 0.0), rd, binv)
    m_rk = each(lambda x, y: jnp.where(incl, _mm_nt(x, y), 0.0), rd, kinv)
    d1 = each(lambda x: jnp.where(same_sub, x, 0.0), a_bk)
    e1 = each(lambda x, y: x - y, a_bk, d1)
    d2 = each(_mm, d1, d1)
    d4 = each(_mm, d2, d2)
    d8 = each(_mm, d4, d4)
    t1 = each(lambda x, y: _mm(eye - x, eye + y), d1, d2)
    t2 = each(lambda x, y: _mm(x, eye + y), t1, d4)
    dinv = each(lambda x, y: _mm(x, eye + y), t2, d8)
    nn = each(_mm, dinv, e1)
    n2 = each(_mm, nn, nn)
    t3 = each(lambda x, y: _mm(eye - x, eye + y), nn, n2)
    tinv = each(_mm, t3, dinv)
    x_st = each(_mm, a_vk, v_st)
    kkdp = each(_mm, tinv, kkd)
    vp = each(_mm, tinv, x_st)
    rq = each(lambda x, y, z: _unstack_heads(x - _mm(y, z)), rd, m_rb, kkdp)
    y_in = each(lambda a1, a2, a3, a4: _unstack_heads(_mm(a1, a2) - _mm(a3, a4)), m_rk, v_st, m_rb, vp)
    phi = each(lambda zl, x, y: eye * jnp.exp(zl) - _mm_tn(x, y), cwl, kkdp, binvw)
    g = each(lambda a1, a2, a3, a4: _mm_tn(a1, a2) - _mm_tn(a3, a4), v_st, kinvw, vp, binvw)
    s_bd = s_ref[...]
    ys = []
    for c in chunks:
        ys.append(_mm_nt(rq[c], s_bd) + y_in[c])
        s_bd = _mm(s_bd, phi[c]) + g[c]
    s_ref[...] = s_bd
    st_ref[0] = _unstack_heads(s_bd)
    y = jnp.concatenate(ys, axis=0) if len(ys) > 1 else ys[0]
    o_ref[...] = _rwkv_finish(y, bonus, gate_ref[...], prm, ones_bd).astype(o_ref.dtype)


def _rwkv_prompt_call(r_sh, gate, lw, nb, seq_len):
    ct = _tile(seq_len, 256)
    ns = seq_len // ct
    x3 = r_sh.reshape(nb, seq_len, RWKV_SHIFT_W)
    sds = jax.ShapeDtypeStruct
    out, st = pl.pallas_call(
        _rwkv_prompt_body, grid=(nb, ns),
        in_specs=[pl.BlockSpec((1, ct, RWKV_SHIFT_W), lambda b, s: (b, s, 0)),
                  pl.BlockSpec((ct, GROUP_W), lambda b, s: (b * ns + s, 0)),
                  _const_spec(lw['mu'].shape), _const_spec(lw['rwkv_prm'].shape)],
        out_specs=[pl.BlockSpec((ct, GROUP_W), lambda b, s: (b * ns + s, 0)),
                   pl.BlockSpec((1, HEAD_DIM, GROUP_W), lambda b, s: (b, 0, 0))],
        out_shape=[sds((nb * seq_len, GROUP_W), BF16), sds((nb, HEAD_DIM, GROUP_W), F32)],
        scratch_shapes=[pltpu.VMEM((8, RWKV_SHIFT_W), F32),
                        pltpu.VMEM((N_HEADS * HEAD_DIM, GROUP_W), F32)],
        compiler_params=_params(("arbitrary", "arbitrary")), name="rwkv_prompt",
    )(x3, gate, lw['mu'], lw['rwkv_prm'])
    st = st.reshape(nb, HEAD_DIM, N_HEADS, HEAD_DIM).transpose(0, 2, 1, 3)
    return out, st


def _rwkv_step_body(x_ref, prev_ref, gate_ref, mu_ref, prm_ref, s_ref, o_ref, so_ref, y_ref):
    tb = x_ref.shape[0]
    x = x_ref[...]
    mixed = x + (prev_ref[...] - x) * mu_ref[...]
    prm = prm_ref[...]
    ones_bd = _head_ones()
    r, k2, v, logw, kk, a, bonus = _rwkv_tokens(mixed, prm, ones_bd)
    w = jnp.exp(logw)
    b = kk * a
    eye = (lax.broadcasted_iota(jnp.int32, (HEAD_DIM, HEAD_DIM), 0)
           == lax.broadcasted_iota(jnp.int32, (HEAD_DIM, HEAD_DIM), 1))
    for bi in range(tb):
        for hh in range(N_HEADS):
            hs = slice(hh * HEAD_DIM, (hh + 1) * HEAD_DIM)
            rowv = lambda z: z[bi:bi + 1, hs]
            st = s_ref[bi, hh]
            sa = jnp.sum(st * rowv(kk), axis=1, keepdims=True)
            v_col = jnp.sum(jnp.where(eye, rowv(v), 0.0), axis=1, keepdims=True)
            st = st * rowv(w) - sa * rowv(b) + v_col * rowv(k2)
            so_ref[bi, hh] = st
            y_col = jnp.sum(st * rowv(r), axis=1, keepdims=True)
            y_ref[bi:bi + 1, hs] = jnp.sum(jnp.where(eye, y_col, 0.0), axis=0, keepdims=True)
    o_ref[...] = _rwkv_finish(y_ref[...], bonus, gate_ref[...], prm, ones_bd).astype(o_ref.dtype)


def _rwkv_step_call(r_sh, prev, gate, state, lw):
    m = r_sh.shape[0]
    tb = _tile(m, 8)
    row = lambda w: pl.BlockSpec((tb, w), lambda i: (i, 0))
    st_spec = pl.BlockSpec((tb, N_HEADS, HEAD_DIM, HEAD_DIM), lambda i: (i, 0, 0, 0))
    sds = jax.ShapeDtypeStruct
    return pl.pallas_call(
        _rwkv_step_body, grid=(m // tb,),
        in_specs=[row(RWKV_SHIFT_W), row(RWKV_SHIFT_W), row(GROUP_W),
                  _const_spec(lw['mu'].shape), _const_spec(lw['rwkv_prm'].shape), st_spec],
        out_specs=[row(GROUP_W), st_spec],
        out_shape=[sds((m, GROUP_W), BF16), sds(state.shape, F32)],
        scratch_shapes=[pltpu.VMEM((tb, GROUP_W), F32)],
        compiler_params=_params(("arbitrary",)), name="rwkv_step",
    )(r_sh, prev, gate, lw['mu'], lw['rwkv_prm'], state)


def _softmax_update(s, v, m_sc, l_sc, acc_sc):
    m_prev = m_sc[...]
    m_new = jnp.maximum(m_prev, jnp.max(s, axis=-1, keepdims=True))
    alpha = jnp.exp(m_prev - m_new)
    p = jnp.exp(s - m_new)
    l_sc[...] = alpha * l_sc[...] + jnp.sum(p, axis=-1, keepdims=True)
    acc_sc[...] = alpha * acc_sc[...] + _mm(p, v)
    m_sc[...] = m_new


def _flash_body(tk, q_ref, k_ref, gate_ref, wuv_ref, o_ref):
    qi = pl.program_id(1)
    tq = q_ref.shape[2]
    per_tk = tk // tq
    n_wide = qi // per_tk

    def block(start, width, carry, masked):
        k = k_ref[0, pl.ds(pl.multiple_of(start, width), width), :]
        new = []
        for hh in range(N_HEADS):
            m_prev, acc = carry[hh]
            s = _mm_nt(q_ref[0, hh], k)
            if masked:
                s = jnp.where(lax.broadcasted_iota(jnp.int32, s.shape, 1)
                              <= lax.broadcasted_iota(jnp.int32, s.shape, 0), s, NEG_INF)
            m_new = jnp.maximum(m_prev, jnp.max(s, axis=-1, keepdims=True))
            p = jnp.exp(s - m_new)
            new.append((m_new, jnp.exp(m_prev - m_new) * acc + _mm(p, k)))
        return tuple(new)

    carry = tuple((jnp.full((tq, 1), NEG_INF, F32), jnp.zeros((tq, QK_W), F32)) for _ in range(N_HEADS))
    carry = lax.fori_loop(0, n_wide, lambda kb, c: block(kb * tk, tk, c, False), carry)
    carry = lax.fori_loop(n_wide * per_tk, qi, lambda kb, c: block(kb * tq, tq, c, False), carry)
    carry = block(qi * tq, tq, carry, True)
    out = jnp.zeros((tq, GROUP_W), F32)
    for hh in range(N_HEADS):
        acc = carry[hh][1]
        o = acc[:, 0:KV_RANK] / acc[:, ONES_LANE:ONES_LANE + 1]
        out = out + _mm(o, wuv_ref[hh])
    o_ref[...] = (out * gate_ref[...]).astype(o_ref.dtype)


def _flash_call(qcat, kcat, gate, w_uv_pad, seq_len):
    nb = qcat.shape[0]
    tq = _tile(seq_len, 256)
    tk = _tile(seq_len, 1024)
    nq = seq_len // tq
    k3 = kcat.reshape(nb, seq_len, QK_W)
    return pl.pallas_call(
        functools.partial(_flash_body, tk), grid=(nb, nq),
        in_specs=[pl.BlockSpec((1, N_HEADS, tq, QK_W), lambda b, i: (b, 0, i, 0)),
                  pl.BlockSpec((1, seq_len, QK_W), lambda b, i: (b, 0, 0)),
                  pl.BlockSpec((tq, GROUP_W), lambda b, i: (b * nq + i, 0)),
                  _const_spec(w_uv_pad.shape)],
        out_specs=pl.BlockSpec((tq, GROUP_W), lambda b, i: (b * nq + i, 0)),
        out_shape=jax.ShapeDtypeStruct((nb * seq_len, GROUP_W), BF16),
        compiler_params=_params(("arbitrary", "arbitrary")), name="flash_mla",
    )(qcat, k3, gate, w_uv_pad)


def _decode_body(layer, nch, pg, pt_ref, q_ref, cnew_ref, knew_ref, gate_ref, wuv_ref, ckv_hbm, kr_hbm,
                 o_ref, cbuf, kbuf, sem, m_sc, l_sc, acc_sc):
    b, c = pl.program_id(0), pl.program_id(1)
    step = b * nch + c
    total = pl.num_programs(0) * nch
    slot = step % 2
    page_len = cbuf.shape[2]

    def page_copies(st, sl):
        bb, cc = st // nch, st % nch
        out = []
        for p in range(pg):
            page = pt_ref[bb, cc * pg + p]
            out.append(pltpu.make_async_copy(ckv_hbm.at[layer, page], cbuf.at[sl, p], sem.at[0, sl]))
            out.append(pltpu.make_async_copy(kr_hbm.at[layer, page], kbuf.at[sl, :, pl.ds(p * page_len, page_len)],
                                             sem.at[1, sl]))
        return out

    @pl.when(step == 0)
    def _():
        for cp in page_copies(0, 0):
            cp.start()

    @pl.when(step + 1 < total)
    def _():
        for cp in page_copies(step + 1, 1 - slot):
            cp.start()

    @pl.when(c == 0)
    def _():
        m_sc[...] = jnp.full_like(m_sc, NEG_INF)
        l_sc[...] = jnp.zeros_like(l_sc)
        acc_sc[...] = jnp.zeros_like(acc_sc)

    for cp in page_copies(step, slot):
        cp.wait()

    ck = cbuf[slot].reshape(pg * page_len, KV_RANK).astype(BF16)
    q = q_ref[0]
    s = _mm_nt(q[:, 0:KV_RANK], ck) + _mm(q[:, KV_RANK:QK_DIM], kbuf[slot])
    _softmax_update(s, ck, m_sc, l_sc, acc_sc)

    @pl.when(c == nch - 1)
    def _():
        qf = q.astype(F32)
        cn, kn = cnew_ref[0], knew_ref[0]
        s_new = (jnp.sum(qf[:, 0:KV_RANK] * cn, axis=-1, keepdims=True)
                 + jnp.sum(qf[:, KV_RANK:QK_DIM] * kn, axis=-1, keepdims=True))
        m_prev = m_sc[...]
        m_new = jnp.maximum(m_prev, s_new)
        alpha, p_new = jnp.exp(m_prev - m_new), jnp.exp(s_new - m_new)
        l_fin = alpha * l_sc[...] + p_new
        o = (alpha * acc_sc[...] + p_new * cn) / l_fin
        out = jnp.zeros((1, GROUP_W), F32)
        for hh in range(N_HEADS):
            out = out + _mm(o, wuv_ref[hh])[hh:hh + 1, :]
        o_ref[0] = out * gate_ref[0]


def _decode_call(layer, page_table, q16, ckv_new, kr_new, gate, w_uv_pad, cache_ckv, cache_krope):
    nb, n_pages = page_table.shape
    page = cache_ckv.shape[2]
    pg = _tile(n_pages, 64)
    nch = n_pages // pg
    blk = lambda shape: pl.BlockSpec((1,) + shape, lambda b, c, pt: (b, 0, 0))
    grid_spec = pltpu.PrefetchScalarGridSpec(
        num_scalar_prefetch=1, grid=(nb, nch),
        in_specs=[blk((16, QK_W)), blk((1, KV_RANK)), blk((1, ROPE_DIM)), blk((1, GROUP_W)),
                  pl.BlockSpec(w_uv_pad.shape, lambda b, c, pt: (0, 0, 0)),
                  pl.BlockSpec(memory_space=pl.ANY), pl.BlockSpec(memory_space=pl.ANY)],
        out_specs=blk((1, GROUP_W)),
        scratch_shapes=[pltpu.VMEM((2, pg, page, KV_RANK), F32), pltpu.VMEM((2, ROPE_DIM, pg * page), F32),
                        pltpu.SemaphoreType.DMA((2, 2)),
                        pltpu.VMEM((16, 1), F32), pltpu.VMEM((16, 1), F32), pltpu.VMEM((16, KV_RANK), F32)])
    out = pl.pallas_call(
        functools.partial(_decode_body, layer, nch, pg), grid_spec=grid_spec,
        out_shape=jax.ShapeDtypeStruct((nb, 1, GROUP_W), F32),
        compiler_params=_params(("arbitrary", "arbitrary")), name="paged_decode",
    )(page_table, q16, ckv_new[:, None, :], kr_new[:, None, :], gate[:, None, :], w_uv_pad,
      cache_ckv, cache_krope)
    return out.reshape(nb, GROUP_W)


def _memattn_body(q_ref, gate_ref, k_ref, v_ref, o_ref):
    tb = q_ref.shape[0]
    head = lax.broadcasted_iota(jnp.int32, (8, GROUP_W), 1) // HEAD_DIM
    sel = head == lax.broadcasted_iota(jnp.int32, (8, GROUP_W), 0)
    for bi in range(tb):
        q_bd = jnp.where(sel, q_ref[bi], 0.0)
        s = _mm(q_bd, k_ref[0, bi])
        p = jnp.exp(s - jnp.max(s, axis=-1, keepdims=True))
        p = p / jnp.sum(p, axis=-1, keepdims=True)
        o = jnp.where(sel, _mm_nt(p, v_ref[0, bi]), 0.0)
        o_ref[bi] = jnp.sum(o, axis=0, keepdims=True) * gate_ref[bi]


def _memattn_call(layer, mem_q, gate, mem_k, mem_v):
    m = mem_q.shape[0]
    n_mem = mem_k.shape[3]
    tb = _tile(m, 4)
    vec = pl.BlockSpec((tb, 1, GROUP_W), lambda i: (i, 0, 0))
    kv = pl.BlockSpec((1, tb, GROUP_W, n_mem), lambda i: (layer, i, 0, 0))
    out = pl.pallas_call(
        _memattn_body, grid=(m // tb,),
        in_specs=[vec, vec, kv, kv], out_specs=vec,
        out_shape=jax.ShapeDtypeStruct((m, 1, GROUP_W), F32),
        compiler_params=_params(("arbitrary",)), name="mem_attn_decode",
    )(mem_q[:, None, :], gate[:, None, :], mem_k, mem_v)
    return out.reshape(m, GROUP_W)


def _outproj_body(x_ref, m0_ref, m1_ref, m2_ref, m3_ref, w_ref, g_ref, o_ref):
    acc = jnp.zeros(x_ref.shape, F32)
    for i, m_ref in enumerate((m0_ref, m1_ref, m2_ref, m3_ref)):
        acc = acc + _mm(m_ref[...], w_ref[i * GROUP_W:(i + 1) * GROUP_W, :])
    o_ref[...] = x_ref[...] + _rms(acc, g_ref[...])


def _outproj_call(x, mixed, w_out, g):
    m, d = x.shape
    tm = _tile(m, 512)
    row = lambda w: pl.BlockSpec((tm, w), lambda i: (i, 0))
    return pl.pallas_call(
        _outproj_body, grid=(m // tm,),
        in_specs=[row(d)] + [row(GROUP_W)] * 4 + [_const_spec(w_out.shape), _const_spec(g.shape)],
        out_specs=row(d), out_shape=jax.ShapeDtypeStruct((m, d), F32),
        compiler_params=_params(("arbitrary",)), name="out_proj",
    )(x, *mixed, w_out, g)


def _prep_layer(l, pre_norm_g, post_norm_g, w_in, w_out, conv_w, rwkv_mu, rwkv_w0, rwkv_w2, rwkv_a0, rwkv_a2,
                rwkv_k_k, rwkv_k_a, rwkv_r_k, rwkv_lnx_g, rwkv_lnx_b, mla_q_norm_g, mla_w_uq, mla_kv_norm_g,
                mla_w_uk, mla_w_uv, mem_norm_g, w_mem_k, w_mem_v):
    d = w_in.shape[1]
    kr0 = _C_MLAG
    w = w_in[l]
    w_in_p = jnp.concatenate([w[:, :kr0], w[:, kr0 + ROPE_DIM:], w[:, kr0:kr0 + ROPE_DIM],
                              jnp.zeros((d, LANES - ROPE_DIM), F32)], axis=1).astype(BF16)
    uq = mla_w_uq[l].reshape(Q_RANK, N_HEADS, NOPE_DIM + ROPE_DIM)
    w_uq = jnp.concatenate([uq[:, :, :NOPE_DIM].reshape(Q_RANK, -1), uq[:, :, NOPE_DIM:].reshape(Q_RANK, -1)],
                           axis=1).astype(BF16)
    w_uv_pad = jnp.zeros((N_HEADS, KV_RANK, GROUP_W), F32)
    for hh in range(N_HEADS):
        w_uv_pad = w_uv_pad.at[hh, :, hh * HEAD_DIM:(hh + 1) * HEAD_DIM].set(mla_w_uv[l][:, hh, :])
    half = ROPE_DIM // 2
    inv = jnp.power(ROPE_BASE, -jnp.arange(half, dtype=F32) / half)
    zl = jnp.zeros((LORA_W, GROUP_W), F32)
    vecs = jnp.stack([rwkv_w0[l], rwkv_a0[l], rwkv_k_k[l], rwkv_k_a[l], rwkv_r_k[l].reshape(-1),
                      rwkv_lnx_g[l], rwkv_lnx_b[l], jnp.zeros((GROUP_W,), F32)])
    return {
        'pre_g': pre_norm_g[l][None], 'post_g': post_norm_g[l][None], 'w_in': w_in_p,
        'w_out': w_out[l].astype(BF16), 'conv_w': conv_w[l], 'q_norm_g': mla_q_norm_g[l][None], 'w_uq': w_uq,
        'kv_norm_g': mla_kv_norm_g[l][None], 'w_ukT': jnp.transpose(mla_w_uk[l], (1, 2, 0)).astype(BF16),
        'w_uv_pad': w_uv_pad.astype(BF16), 'inv_full': jnp.tile(inv, LANES // half)[None],
        'mu': rwkv_mu[l][None],
        'rwkv_prm': jnp.concatenate([vecs, rwkv_w2[l], zl, zl, rwkv_a2[l]], axis=0),
        'mem_g': mem_norm_g[l][None],
        'w_mem_kv': jnp.concatenate([w_mem_k[l], w_mem_v[l]], axis=1).astype(BF16),
    }


def kernel(x_prompt, x_sample, cache_ckv, cache_krope, cache_mem_k, cache_mem_v, state_conv, state_rwkv_shift, state_rwkv, page_table, mem_prompt, pre_norm_g, post_norm_g, w_in, w_out, conv_w, rwkv_mu, rwkv_w0, rwkv_w2, rwkv_a0, rwkv_a2, rwkv_k_k, rwkv_k_a, rwkv_r_k, rwkv_lnx_g, rwkv_lnx_b, mla_q_norm_g, mla_w_uq, mla_kv_norm_g, mla_w_uk, mla_w_uv, mem_norm_g, w_mem_k, w_mem_v):
    bp, sp, d = x_prompt.shape
    bs, ts, _ = x_sample.shape
    assert ts == 1
    depth = w_in.shape[0]
    n_mem = mem_prompt.shape[1]
    past_len = page_table.shape[1] * cache_ckv.shape[2]
    mem_k4 = jnp.transpose(cache_mem_k, (0, 1, 3, 4, 2)).reshape(depth, bs, GROUP_W, n_mem)
    mem_v4 = jnp.transpose(cache_mem_v, (0, 1, 3, 4, 2)).reshape(depth, bs, GROUP_W, n_mem)
    cache_krope_t = jnp.swapaxes(cache_krope, 2, 3)
    xp = x_prompt.reshape(bp * sp, d)
    xs = x_sample.reshape(bs, d)
    mem2 = mem_prompt.reshape(bp * n_mem, d)
    outs = [[] for _ in range(12)]
    for l in range(depth):
        lw = _prep_layer(l, pre_norm_g, post_norm_g, w_in, w_out, conv_w, rwkv_mu, rwkv_w0, rwkv_w2, rwkv_a0,
                         rwkv_a2, rwkv_k_k, rwkv_k_a, rwkv_r_k, rwkv_lnx_g, rwkv_lnx_b, mla_q_norm_g, mla_w_uq,
                         mla_kv_norm_g, mla_w_uk, mla_w_uv, mem_norm_g, w_mem_k, w_mem_v)
        mk, mv, mk_b, mv_b = _memkv_call(mem2, lw['mem_g'], lw['w_mem_kv'])
        (m_conv, conv_p, r_sh, r_gate, ckv, kr, kcat, qcat, mla_gate, m_mem) = _proj_call(
            xp, lw, decode=False, seq_len=sp,
            mk=mk_b.reshape(bp, n_mem, GROUP_W), mv=mv_b.reshape(bp, n_mem, GROUP_W))
        m_rwkv, st_p = _rwkv_prompt_call(r_sh, r_gate, lw, bp, sp)
        m_mla = _flash_call(qcat, kcat, mla_gate, lw['w_uv_pad'], sp)
        xp = _outproj_call(xp, (m_conv, m_rwkv, m_mla, m_mem), lw['w_out'], lw['post_g'])
        sh_p = r_sh.reshape(bp, sp, RWKV_SHIFT_W)[:, -1]
        (s_conv, cn0, cn1, r_sh_s, r_gate_s, ckv_s, kr_s, _, qcat_s, mla_gate_s, mem_q_s, mem_gate_s) = _proj_call(
            xs, lw, decode=True, seq_len=1, pos0=float(past_len), conv_state=state_conv[l])
        s_rwkv, st_s = _rwkv_step_call(r_sh_s, state_rwkv_shift[l], r_gate_s, state_rwkv[l], lw)
        q16 = jnp.pad(jnp.transpose(qcat_s[0], (1, 0, 2)), ((0, 0), (0, 16 - N_HEADS), (0, 0)))
        s_mla = _decode_call(l, page_table, q16, ckv_s, kr_s, mla_gate_s, lw['w_uv_pad'], cache_ckv, cache_krope_t)
        s_mem = _memattn_call(l, mem_q_s, mem_gate_s, mem_k4, mem_v4)
        xs = _outproj_call(xs, (s_conv, s_rwkv, s_mla, s_mem), lw['w_out'], lw['post_g'])
        vals = (ckv.reshape(bp, sp, KV_RANK), ckv_s.reshape(bs, ts, KV_RANK),
                kr.reshape(bp, sp, ROPE_DIM), kr_s.reshape(bs, ts, ROPE_DIM),
                mk.reshape(bp, n_mem, N_HEADS, HEAD_DIM), mv.reshape(bp, n_mem, N_HEADS, HEAD_DIM),
                conv_p, jnp.stack([cn0, cn1], axis=1), sh_p, r_sh_s, st_p, st_s)
        for o, v in zip(outs, vals):
            o.append(v)
    return (xp.reshape(bp, sp, d), xs.reshape(bs, ts, d)) + tuple(jnp.stack(o) for o in outs)
```

```python
import functools

import jax
import jax.numpy as jnp
import numpy as np
from jax import lax
from jax.experimental import pallas as pl
from jax.experimental.pallas import tpu as pltpu

F32 = jnp.float32
BF16 = jnp.bfloat16

HEAD_DIM = 64
N_HEADS = 4
GROUP_W = N_HEADS * HEAD_DIM
CONV_K = 3
LORA_W = 64
RWKV_SHIFT_W = 3 * GROUP_W + 2 * LORA_W
Q_RANK = 256
KV_RANK = 128
NOPE_DIM = 64
ROPE_DIM = 32
QK_DIM = KV_RANK + ROPE_DIM
QK_W = 2 * KV_RANK
ONES_LANE = QK_DIM
RMS_EPS = 1e-6
LNX_EPS = 64e-5
ROPE_BASE = 10000.0
DECAY_SCALE = 0.6065306597
MLA_SCALE = (NOPE_DIM + ROPE_DIM) ** -0.5
MEM_SCALE = HEAD_DIM ** -0.5
NEG_INF = -1e30

LANES = 128
CHUNK = 64
SUB = 16
VMEM_LIMIT = 56 * 1024 * 1024

_C_CONV, _C_RSH, _C_RGATE, _C_QDOWN, _C_CKV, _C_MLAG, _C_MEMQ, _C_MEMG, _C_KR, _C_END = (
    0, 1024, 1920, 2176, 2432, 2560, 2816, 3072, 3328, 3456)


def _mm(a, b):
    return jnp.dot(a.astype(BF16), b.astype(BF16), preferred_element_type=F32)


def _mm_nt(a, b):
    return lax.dot_general(a.astype(BF16), b.astype(BF16), (((1,), (1,)), ((), ())),
                           preferred_element_type=F32)


def _mm_tn(a, b):
    return lax.dot_general(a.astype(BF16), b.astype(BF16), (((0,), (0,)), ((), ())),
                           preferred_element_type=F32)


def _split_bf16(x, n):
    parts = []
    for _ in range(n):
        p = x.astype(BF16)
        parts.append(p)
        x = x - p.astype(F32)
    return parts


def _mm_hi(a, b):
    (a1, a2), (b1, b2) = _split_bf16(a, 2), _split_bf16(b, 2)
    dot = lambda x, y: jnp.dot(x, y, preferred_element_type=F32)
    return dot(a1, b1) + (dot(a1, b2) + dot(a2, b1))


def _mm_exact_lhs(a01, b):
    a = a01.astype(BF16)
    b1, b2, b3 = _split_bf16(b, 3)
    dot = lambda x: jnp.dot(a, x, preferred_element_type=F32)
    return dot(b1) + (dot(b2) + dot(b3))


def _sigmoid(x):
    return 1.0 / (1.0 + jnp.exp(-x))


def _silu(x):
    return x * _sigmoid(x)


def _rms(x, g):
    return x * lax.rsqrt(jnp.mean(x * x, axis=-1, keepdims=True) + RMS_EPS) * g


def _params(sem):
    return pltpu.CompilerParams(dimension_semantics=sem, vmem_limit_bytes=VMEM_LIMIT)


def _const_spec(shape):
    nd = len(shape)
    return pl.BlockSpec(shape, lambda *_: (0,) * nd)


def _tile(n, pref):
    t = min(n, pref)
    assert n % t == 0, (n, pref)
    return t


def _proj_body(decode, tiles_per_seq, pos0, *refs):
    if decode:
        (x_ref, g_ref, w_ref, cw_ref, qg_ref, wuq_ref, kvg_ref, wuk_ref, inv_ref, cb0_ref, cb1_ref,
         mconv_ref, cn0_ref, cn1_ref, rsh_ref, rgate_ref, ckv_ref, kr_ref, kcat_ref, qcat_ref,
         mlag_ref, memq_ref, memg_ref) = refs
    else:
        (x_ref, g_ref, w_ref, cw_ref, qg_ref, wuq_ref, kvg_ref, wuk_ref, inv_ref, mk_ref, mv_ref,
         mconv_ref, cst_ref, rsh_ref, rgate_ref, ckv_ref, kr_ref, kcat_ref, qcat_ref,
         mlag_ref, mmem_ref, carry_ref, cos_row_ref, sin_row_ref) = refs
    tm = x_ref.shape[0]
    t = pl.program_id(0) % tiles_per_seq
    h = _rms(x_ref[...], g_ref[...]).astype(BF16)

    def proj(lo, hi):
        return jnp.dot(h, w_ref[:, lo:hi], preferred_element_type=F32)

    cv = proj(_C_CONV, _C_RSH)
    c_b, c_c, c_x, c_g = (cv[:, i * GROUP_W:(i + 1) * GROUP_W] for i in range(4))
    u = c_c * c_x
    w0, w1, w2 = cw_ref[0:1, :], cw_ref[1:2, :], cw_ref[2:3, :]
    if decode:
        b0, b1 = cb0_ref[...], cb1_ref[...]
        y = b0 * w0 + b1 * w1 + u * w2
        cn0_ref[...] = b1
        cn1_ref[...] = u
    else:
        @pl.when(t == 0)
        def _():
            carry_ref[...] = jnp.zeros_like(carry_ref)
        p0, p1 = carry_ref[0:1, :], carry_ref[1:2, :]
        row = lax.broadcasted_iota(jnp.int32, u.shape, 0)
        u1 = jnp.where(row == 0, p1, pltpu.roll(u, 1, 0))
        u2 = jnp.where(row == 0, p0, jnp.where(row == 1, p1, pltpu.roll(u, 2, 0)))
        y = u2 * w0 + u1 * w1 + u * w2
        carry_ref[0:2, :] = u[tm - 2:tm, :]
        cst_ref[0] = u[tm - 2:tm, :]
    mconv_ref[...] = (c_b * y * _silu(c_g)).astype(mconv_ref.dtype)

    rsh_ref[...] = proj(_C_RSH, _C_RGATE)
    rgate_ref[...] = _silu(proj(_C_RGATE, _C_QDOWN))

    if decode:
        ang = jnp.full((tm, 1), pos0, F32) * inv_ref[...]
        cos, sin = jnp.cos(ang), jnp.sin(ang)
    else:
        @pl.when(pl.program_id(0) == 0)
        def _():
            ang_row = lax.broadcasted_iota(jnp.int32, (tm, 1), 0).astype(F32) * inv_ref[...]
            cos_row_ref[...] = jnp.cos(ang_row)
            sin_row_ref[...] = jnp.sin(ang_row)
        ang_tile = (t * tm).astype(F32) * inv_ref[...]
        cos_t, sin_t = jnp.cos(ang_tile), jnp.sin(ang_tile)
        cos_r, sin_r = cos_row_ref[...], sin_row_ref[...]
        cos = cos_t * cos_r - sin_t * sin_r
        sin = sin_t * cos_r + cos_t * sin_r
    first = (lax.broadcasted_iota(jnp.int32, (tm, LANES), 1) % ROPE_DIM) < (ROPE_DIM // 2)

    def rope(v):
        partner = jnp.where(first, -pltpu.roll(v, LANES - ROPE_DIM // 2, 1), pltpu.roll(v, ROPE_DIM // 2, 1))
        return v * cos + partner * sin

    qn = _rms(proj(_C_QDOWN, _C_CKV), qg_ref[...])
    q = _mm(qn, wuq_ref[...])
    q_rope = rope(q[:, GROUP_W:GROUP_W + LANES]) * MLA_SCALE
    lane = lax.broadcasted_iota(jnp.int32, (tm, LANES), 1)
    for hh in range(N_HEADS):
        q_lat = _mm(q[:, hh * NOPE_DIM:(hh + 1) * NOPE_DIM], wuk_ref[hh]) * MLA_SCALE
        qcat_ref[0, hh, :, 0:KV_RANK] = q_lat.astype(qcat_ref.dtype)
        q_r = q_rope if hh == 0 else pltpu.roll(q_rope, LANES - hh * ROPE_DIM, 1)
        qcat_ref[0, hh, :, KV_RANK:QK_W] = jnp.where(lane < ROPE_DIM, q_r, 0.0).astype(qcat_ref.dtype)
    ckv = _rms(proj(_C_CKV, _C_MLAG), kvg_ref[...])
    krope = rope(proj(_C_KR, _C_END))
    ckv_ref[...] = ckv
    kr_ref[...] = krope[:, 0:ROPE_DIM]
    kcat_ref[:, 0:KV_RANK] = ckv.astype(kcat_ref.dtype)
    kcat_ref[:, KV_RANK:QK_W] = jnp.where(lane == ONES_LANE - KV_RANK, 1.0, krope).astype(kcat_ref.dtype)
    mlag_ref[...] = _silu(proj(_C_MLAG, _C_MEMQ))

    mq = proj(_C_MEMQ, _C_MEMG) * MEM_SCALE
    mg = _silu(proj(_C_MEMG, _C_KR))
    if decode:
        memq_ref[...] = mq
        memg_ref[...] = mg
    else:
        mk, mv = mk_ref[0], mv_ref[0]
        head = lax.broadcasted_iota(jnp.int32, (tm, GROUP_W), 1) // HEAD_DIM
        acc = jnp.zeros((tm, GROUP_W), F32)
        for hh in range(N_HEADS):
            s = _mm_nt(jnp.where(head == hh, mq, 0.0), mk)
            p = jnp.exp(s - jnp.max(s, axis=-1, keepdims=True))
            p = p / jnp.sum(p, axis=-1, keepdims=True)
            acc = acc + jnp.where(head == hh, _mm(p, mv), 0.0)
        mmem_ref[...] = (acc * mg).astype(mmem_ref.dtype)


def _proj_call(x, lw, *, decode, seq_len, pos0=0.0, conv_state=None, mk=None, mv=None):
    m, d = x.shape
    tm = _tile(seq_len if not decode else m, 512)
    tps = 1 if decode else seq_len // tm
    nb = m // (tm * tps)
    grid = (m // tm,)
    row = lambda w: pl.BlockSpec((tm, w), lambda i: (i, 0))
    consts = [lw['pre_g'], lw['w_in'], lw['conv_w'], lw['q_norm_g'], lw['w_uq'], lw['kv_norm_g'],
              lw['w_ukT'], lw['inv_full']]
    in_specs = [row(d)] + [_const_spec(c.shape) for c in consts]
    qcat_shape = (nb, N_HEADS, tm * tps, QK_W)
    qcat_spec = pl.BlockSpec((1, N_HEADS, tm, QK_W), lambda i: (i // tps, 0, i % tps, 0))
    sds = jax.ShapeDtypeStruct
    common_out = [
        (sds((m, RWKV_SHIFT_W), F32), row(RWKV_SHIFT_W)),
        (sds((m, GROUP_W), F32), row(GROUP_W)),
        (sds((m, KV_RANK), F32), row(KV_RANK)),
        (sds((m, ROPE_DIM), F32), row(ROPE_DIM)),
        (sds((m, QK_W), BF16), row(QK_W)),
        (sds(qcat_shape, BF16), qcat_spec),
        (sds((m, GROUP_W), F32), row(GROUP_W)),
    ]
    if decode:
        ins = [x] + consts + [conv_state[:, 0], conv_state[:, 1]]
        in_specs += [row(GROUP_W), row(GROUP_W)]
        outs = ([(sds((m, GROUP_W), BF16), row(GROUP_W)),
                 (sds((m, GROUP_W), F32), row(GROUP_W)), (sds((m, GROUP_W), F32), row(GROUP_W))]
                + common_out
                + [(sds((m, GROUP_W), F32), row(GROUP_W)), (sds((m, GROUP_W), F32), row(GROUP_W))])
        scratch = []
    else:
        ins = [x] + consts + [mk, mv]
        n_mem = mk.shape[1]
        in_specs += [pl.BlockSpec((1, n_mem, GROUP_W), lambda i: (i // tps, 0, 0))] * 2
        outs = ([(sds((m, GROUP_W), BF16), row(GROUP_W)),
                 (sds((nb, CONV_K - 1, GROUP_W), F32),
                  pl.BlockSpec((1, CONV_K - 1, GROUP_W), lambda i: (i // tps, 0, 0)))]
                + common_out
                + [(sds((m, GROUP_W), BF16), row(GROUP_W))])
        scratch = [pltpu.VMEM((8, GROUP_W), F32), pltpu.VMEM((tm, LANES), F32), pltpu.VMEM((tm, LANES), F32)]
    return pl.pallas_call(
        functools.partial(_proj_body, decode, tps, pos0),
        grid=grid, in_specs=in_specs,
        out_specs=[o[1] for o in outs], out_shape=[o[0] for o in outs],
        scratch_shapes=scratch, compiler_params=_params(("arbitrary",)),
        name="proj_decode" if decode else "proj_prompt",
    )(*ins)


def _memkv_body(x_ref, g_ref, w_ref, k_ref, v_ref, kb_ref, vb_ref):
    kv = _mm(_rms(x_ref[...], g_ref[...]), w_ref[...])
    k, v = kv[:, :GROUP_W], kv[:, GROUP_W:]
    k_ref[...] = k
    v_ref[...] = v
    kb_ref[...] = k.astype(BF16)
    vb_ref[...] = v.astype(BF16)


def _memkv_call(mem, g, w_kv):
    m, d = mem.shape
    tm = _tile(m, 512)
    row = lambda w: pl.BlockSpec((tm, w), lambda i: (i, 0))
    sds = jax.ShapeDtypeStruct
    return pl.pallas_call(
        _memkv_body, grid=(m // tm,),
        in_specs=[row(d), _const_spec(g.shape), _const_spec(w_kv.shape)],
        out_specs=[row(GROUP_W)] * 4,
        out_shape=[sds((m, GROUP_W), F32), sds((m, GROUP_W), F32), sds((m, GROUP_W), BF16), sds((m, GROUP_W), BF16)],
        compiler_params=_params(("arbitrary",)), name="mem_kv",
    )(mem, g, w_kv)


def _head_ones():
    r = lax.broadcasted_iota(jnp.int32, (GROUP_W, GROUP_W), 0) // HEAD_DIM
    c = lax.broadcasted_iota(jnp.int32, (GROUP_W, GROUP_W), 1) // HEAD_DIM
    return (r == c).astype(F32)


def _rwkv_tokens(mixed, prm, ones_bd):
    w0, a0, k_k, k_a, r_k = (prm[i:i + 1, :] for i in range(5))
    w2p, a2p = prm[8:8 + LANES, :], prm[8 + LANES:8 + 2 * LANES, :]
    r, k, v = mixed[:, 0:GROUP_W], mixed[:, GROUP_W:2 * GROUP_W], mixed[:, 2 * GROUP_W:3 * GROUP_W]
    wa = mixed[:, 3 * GROUP_W:RWKV_SHIFT_W]
    logw = -DECAY_SCALE * _sigmoid(w0 + _mm_hi(jnp.tanh(wa), w2p))
    a = _sigmoid(a0 + _mm_hi(wa, a2p))
    kkr = k * k_k
    kk = kkr * lax.rsqrt(_mm(kkr * kkr, ones_bd) + 1e-12)
    k2 = k * (1.0 + (a - 1.0) * k_a)
    bonus = _mm(r * k2 * r_k, ones_bd) * v
    return r, k2, v, logw, kk, a, bonus


def _rwkv_finish(y, bonus, gate, prm, ones_bd):
    lnx_g, lnx_b = prm[5:6, :], prm[6:7, :]
    mean = _mm(y, ones_bd) * (1.0 / HEAD_DIM)
    yc = y - mean
    var = _mm(yc * yc, ones_bd) * (1.0 / HEAD_DIM)
    return ((yc * lax.rsqrt(var + LNX_EPS)) * lnx_g + lnx_b + bonus) * gate


def _stack_heads(x, head_lane):
    return jnp.concatenate([jnp.where(head_lane == hh, x, 0.0) for hh in range(N_HEADS)], axis=0)


def _unstack_heads(x):
    c = x.shape[0] // N_HEADS
    return x[0:c] + x[c:2 * c] + x[2 * c:3 * c] + x[3 * c:4 * c]


def _rwkv_prompt_body(x_ref, gate_ref, mu_ref, prm_ref, o_ref, st_ref, prev_ref, s_ref):
    ct = x_ref.shape[1]
    si = pl.program_id(1)

    @pl.when(si == 0)
    def _():
        prev_ref[...] = jnp.zeros_like(prev_ref)
        s_ref[...] = jnp.zeros_like(s_ref)

    x = x_ref[0]
    row = lax.broadcasted_iota(jnp.int32, x.shape, 0)
    prev = jnp.where(row == 0, prev_ref[0:1, :], pltpu.roll(x, 1, 0))
    prev_ref[0:1, :] = x[ct - 1:ct, :]
    mixed = x + (prev - x) * mu_ref[...]
    prm = prm_ref[...]
    ones_bd = _head_ones()
    r, k2, v, logw, kk, a, bonus = _rwkv_tokens(mixed, prm, ones_bd)
    b = kk * a

    n = N_HEADS * CHUNK
    head_lane = lax.broadcasted_iota(jnp.int32, (CHUNK, GROUP_W), 1) // HEAD_DIM
    rr = lax.broadcasted_iota(jnp.int32, (n, n), 0)
    cc = lax.broadcasted_iota(jnp.int32, (n, n), 1)
    same_sub = (rr // SUB) == (cc // SUB)
    same_head = (rr // CHUNK) == (cc // CHUNK)
    eye = (rr == cc).astype(F32)
    t_w = lax.broadcasted_iota(jnp.int32, (CHUNK, GROUP_W), 0)
    s_w = lax.broadcasted_iota(jnp.int32, (CHUNK, GROUP_W), 1) % CHUNK
    strict_w, incl_w = t_w > s_w, t_w >= s_w
    eye_w = (t_w == s_w).astype(F32)
    eye_sub = (lax.broadcasted_iota(jnp.int32, (SUB, GROUP_W), 0)
               == lax.broadcasted_iota(jnp.int32, (SUB, GROUP_W), 1) % SUB).astype(F32)
    tri = (lax.broadcasted_iota(jnp.int32, (CHUNK, CHUNK), 0)
           >= lax.broadcasted_iota(jnp.int32, (CHUNK, CHUNK), 1)).astype(F32)

    def sub_square(z):
        return jnp.where(same_sub, jnp.concatenate([z] * (n // SUB), axis=0), 0.0)

    def sub_strip(z):
        out = z[0:SUB]
        for i in range(1, n // SUB):
            out = out + z[i * SUB:(i + 1) * SUB]
        return out

    chunks = range(ct // CHUNK)
    each = lambda f, *xs: [f(*(x[c] for x in xs)) for c in chunks]
    rows = lambda z: [z[c * CHUNK:(c + 1) * CHUNK] for c in chunks]
    stack = lambda z: _stack_heads(z, head_lane)
    lw = rows(logw)
    cw = each(lambda l: _mm_exact_lhs(tri, l), lw)
    cwl = each(lambda z: z[CHUNK - 1:CHUNK, :], cw)
    e_ng = each(lambda z: jnp.exp(-z), cw)
    e_w = each(lambda z, zl: jnp.exp(zl - z), cw, cwl)
    kkd = each(lambda x, z, l: x * jnp.exp(z - l), rows(kk), cw, lw)
    rd = each(lambda x, z: x * jnp.exp(z), rows(r), cw)
    kinvw = each(lambda x, e: x * e, rows(k2), e_w)
    binvw = each(lambda x, e: x * e, rows(b), e_w)
    v_w = rows(v)
    kkd_sq = each(stack, kkd)
    kinv_sq = each(lambda x, e: stack(x * e), rows(k2), e_ng)
    binv_sq = each(lambda x, e: stack(x * e), rows(b), e_ng)
    v_sq = each(stack, v_w)
    a_bk = each(lambda x, y: jnp.where(strict_w, _mm_nt(x, y), 0.0), kkd, binv_sq)
    a_vk = each(lambda x, y: jnp.where(strict_w, _mm_nt(x, y), 0.0), kkd, kinv_sq)
    m_rb = each(lambda x, y: jnp.where(incl_w, _mm_nt(x, y), 0.0), rd, binv_sq)
    m_rk = each(lambda x, y: jnp.where(incl_w, _mm_nt(x, y), 0.0), rd, kinv_sq)
    a_sq = each(stack, a_bk)
    d1_sq = each(lambda x: jnp.where(same_sub, x, 0.0), a_sq)
    e1_sq = each(lambda x, y: x - y, a_sq, d1_sq)
    d1 = each(sub_strip, d1_sq)
    d2 = each(_mm, d1, d1_sq)
    d2_sq = each(sub_square, d2)
    d4 = each(_mm, d2, d2_sq)
    d4_sq = each(sub_square, d4)
    d8 = each(_mm, d4, d4_sq)
    t1 = each(lambda x, y: _mm(eye_sub - x, eye + y), d1, d2_sq)
    t2 = each(lambda x, y: _mm(x, eye + y), t1, d4_sq)
    dinv = each(lambda x, y: _mm(x, eye + sub_square(y)), t2, d8)
    dinv_sq = each(sub_square, dinv)
    dinv_w = each(_unstack_heads, dinv_sq)
    nn = each(_mm, dinv_w, e1_sq)
    nn_sq = each(stack, nn)
    n2 = each(_mm, nn, nn_sq)
    t3 = each(lambda x, y: _mm(eye_w - x, eye + stack(y)), nn, n2)
    tinv = each(_mm, t3, dinv_sq)
    x_w = each(_mm, a_vk, v_sq)
    kkdp = each(_mm, tinv, kkd_sq)
    vp = each(_mm, tinv, each(stack, x_w))
    kkdp_sq = each(stack, kkdp)
    rq = each(lambda x, y, z: x - _mm(y, z), rd, m_rb, kkdp_sq)
    y_in = each(lambda a1, a2, a3, a4: _mm(a1, a2) - _mm(a3, stack(a4)), m_rk, v_sq, m_rb, vp)
    phi = each(lambda zl, x, y: eye * jnp.exp(zl) - jnp.where(same_head, _mm_tn(x, y), 0.0), cwl, kkdp, binvw)
    g = each(lambda a1, a2, a3, a4: _unstack_heads(jnp.where(
        same_head, _mm_tn(jnp.concatenate([a1, -a3], axis=0), jnp.concatenate([a2, a4], axis=0)), 0.0)),
        v_w, kinvw, vp, binvw)
    state = s_ref[...]
    ys = []
    for c in chunks:
        ys.append(_mm_nt(rq[c], stack(state)) + y_in[c])
        state = _mm(state, phi[c]) + g[c]
    s_ref[...] = state
    st_ref[0] = state
    y = jnp.concatenate(ys, axis=0) if len(ys) > 1 else ys[0]
    o_ref[...] = _rwkv_finish(y, bonus, gate_ref[...], prm, ones_bd).astype(o_ref.dtype)


def _rwkv_prompt_call(r_sh, gate, lw, nb, seq_len):
    ct = _tile(seq_len, 512)
    ns = seq_len // ct
    x3 = r_sh.reshape(nb, seq_len, RWKV_SHIFT_W)
    sds = jax.ShapeDtypeStruct
    out, st = pl.pallas_call(
        _rwkv_prompt_body, grid=(nb, ns),
        in_specs=[pl.BlockSpec((1, ct, RWKV_SHIFT_W), lambda b, s: (b, s, 0)),
                  pl.BlockSpec((ct, GROUP_W), lambda b, s: (b * ns + s, 0)),
                  _const_spec(lw['mu'].shape), _const_spec(lw['rwkv_prm'].shape)],
        out_specs=[pl.BlockSpec((ct, GROUP_W), lambda b, s: (b * ns + s, 0)),
                   pl.BlockSpec((1, HEAD_DIM, GROUP_W), lambda b, s: (b, 0, 0))],
        out_shape=[sds((nb * seq_len, GROUP_W), BF16), sds((nb, HEAD_DIM, GROUP_W), F32)],
        scratch_shapes=[pltpu.VMEM((8, RWKV_SHIFT_W), F32),
                        pltpu.VMEM((HEAD_DIM, GROUP_W), F32)],
        compiler_params=_params(("arbitrary", "arbitrary")), name="rwkv_prompt",
    )(x3, gate, lw['mu'], lw['rwkv_prm'])
    st = st.reshape(nb, HEAD_DIM, N_HEADS, HEAD_DIM).transpose(0, 2, 1, 3)
    return out, st


def _rwkv_step_body(x_ref, prev_ref, gate_ref, mu_ref, prm_ref, s_ref, o_ref, so_ref, y_ref):
    tb = x_ref.shape[0]
    x = x_ref[...]
    mixed = x + (prev_ref[...] - x) * mu_ref[...]
    prm = prm_ref[...]
    ones_bd = _head_ones()
    r, k2, v, logw, kk, a, bonus = _rwkv_tokens(mixed, prm, ones_bd)
    w = jnp.exp(logw)
    b = kk * a
    eye = (lax.broadcasted_iota(jnp.int32, (HEAD_DIM, HEAD_DIM), 0)
           == lax.broadcasted_iota(jnp.int32, (HEAD_DIM, HEAD_DIM), 1))
    for bi in range(tb):
        for hh in range(N_HEADS):
            hs = slice(hh * HEAD_DIM, (hh + 1) * HEAD_DIM)
            rowv = lambda z: z[bi:bi + 1, hs]
            st = s_ref[bi, hh]
            sa = jnp.sum(st * rowv(kk), axis=1, keepdims=True)
            v_col = jnp.sum(jnp.where(eye, rowv(v), 0.0), axis=1, keepdims=True)
            st = st * rowv(w) - sa * rowv(b) + v_col * rowv(k2)
            so_ref[bi, hh] = st
            y_col = jnp.sum(st * rowv(r), axis=1, keepdims=True)
            y_ref[bi:bi + 1, hs] = jnp.sum(jnp.where(eye, y_col, 0.0), axis=0, keepdims=True)
    o_ref[...] = _rwkv_finish(y_ref[...], bonus, gate_ref[...], prm, ones_bd).astype(o_ref.dtype)


def _rwkv_step_call(r_sh, prev, gate, state, lw):
    m = r_sh.shape[0]
    tb = _tile(m, 8)
    row = lambda w: pl.BlockSpec((tb, w), lambda i: (i, 0))
    st_spec = pl.BlockSpec((tb, N_HEADS, HEAD_DIM, HEAD_DIM), lambda i: (i, 0, 0, 0))
    sds = jax.ShapeDtypeStruct
    return pl.pallas_call(
        _rwkv_step_body, grid=(m // tb,),
        in_specs=[row(RWKV_SHIFT_W), row(RWKV_SHIFT_W), row(GROUP_W),
                  _const_spec(lw['mu'].shape), _const_spec(lw['rwkv_prm'].shape), st_spec],
        out_specs=[row(GROUP_W), st_spec],
        out_shape=[sds((m, GROUP_W), BF16), sds(state.shape, F32)],
        scratch_shapes=[pltpu.VMEM((tb, GROUP_W), F32)],
        compiler_params=_params(("arbitrary",)), name="rwkv_step",
    )(r_sh, prev, gate, lw['mu'], lw['rwkv_prm'], state)


def _softmax_update(s, v, m_sc, l_sc, acc_sc):
    m_prev = m_sc[...]
    m_new = jnp.maximum(m_prev, jnp.max(s, axis=-1, keepdims=True))
    alpha = jnp.exp(m_prev - m_new)
    p = jnp.exp(s - m_new)
    l_sc[...] = alpha * l_sc[...] + jnp.sum(p, axis=-1, keepdims=True)
    acc_sc[...] = alpha * acc_sc[...] + _mm(p, v)
    m_sc[...] = m_new


def _flash_body(q_ref, k_ref, gate_ref, wuv_ref, o_ref, sa_ref, sb_ref, m_ref, acc_ref):
    qi = pl.program_id(1)
    tq = q_ref.shape[2]

    def keys(kb):
        return k_ref[0, pl.ds(pl.multiple_of(kb * tq, tq), tq), :]

    def scores(kb, s_ref):
        k = keys(kb)
        for hh in range(N_HEADS):
            s_ref[hh] = _mm_nt(q_ref[0, hh], k)

    def update(s_ref, kb, masked):
        k = keys(kb)
        for hh in range(N_HEADS):
            s_h = s_ref[hh]
            if masked:
                s_h = jnp.where(lax.broadcasted_iota(jnp.int32, s_h.shape, 1)
                                <= lax.broadcasted_iota(jnp.int32, s_h.shape, 0), s_h, NEG_INF)
            m_prev = m_ref[hh]
            m_new = jnp.maximum(m_prev, jnp.max(s_h, axis=-1, keepdims=True))
            p = jnp.exp(s_h - m_new)
            acc_ref[hh] = jnp.exp(m_prev - m_new) * acc_ref[hh] + _mm(p, k)
            m_ref[hh] = m_new

    def finish(s_ref):
        update(s_ref, qi, True)
        out = jnp.zeros((tq, GROUP_W), F32)
        for hh in range(N_HEADS):
            acc = acc_ref[hh]
            o = acc[:, 0:KV_RANK] / acc[:, ONES_LANE:ONES_LANE + 1]
            out = out + _mm(o, wuv_ref[hh])
        o_ref[...] = (out * gate_ref[...]).astype(o_ref.dtype)

    m_ref[...] = jnp.full_like(m_ref, NEG_INF)
    acc_ref[...] = jnp.zeros_like(acc_ref)
    scores(0, sa_ref)

    def pair(j, _):
        kb = 2 * j
        scores(kb + 1, sb_ref)
        update(sa_ref, kb, False)
        scores(kb + 2, sa_ref)
        update(sb_ref, kb + 1, False)
        return 0

    lax.fori_loop(0, qi // 2, pair, 0)

    @pl.when(qi % 2 == 1)
    def _():
        scores(qi, sb_ref)
        update(sa_ref, qi - 1, False)
        finish(sb_ref)

    @pl.when(qi % 2 == 0)
    def _():
        finish(sa_ref)


def _flash_call(qcat, kcat, gate, w_uv_pad, seq_len):
    nb = qcat.shape[0]
    tq = _tile(seq_len, 512)
    nq = seq_len // tq
    k3 = kcat.reshape(nb, seq_len, QK_W)
    return pl.pallas_call(
        _flash_body, grid=(nb, nq),
        in_specs=[pl.BlockSpec((1, N_HEADS, tq, QK_W), lambda b, i: (b, 0, i, 0)),
                  pl.BlockSpec((1, seq_len, QK_W), lambda b, i: (b, 0, 0)),
                  pl.BlockSpec((tq, GROUP_W), lambda b, i: (b * nq + i, 0)),
                  _const_spec(w_uv_pad.shape)],
        out_specs=pl.BlockSpec((tq, GROUP_W), lambda b, i: (b * nq + i, 0)),
        out_shape=jax.ShapeDtypeStruct((nb * seq_len, GROUP_W), BF16),
        scratch_shapes=[pltpu.VMEM((N_HEADS, tq, tq), F32), pltpu.VMEM((N_HEADS, tq, tq), F32),
                        pltpu.VMEM((N_HEADS, tq, 1), F32), pltpu.VMEM((N_HEADS, tq, QK_W), F32)],
        compiler_params=_params(("arbitrary", "arbitrary")), name="flash_mla",
    )(qcat, k3, gate, w_uv_pad)


def _decode_body(layer, nch, pg, pt_ref, q_ref, cnew_ref, knew_ref, gate_ref, wuv_ref, ckv_hbm, kr_hbm,
                 o_ref, cbuf, kbuf, sem, m_sc, l_sc, acc_sc):
    b, c = pl.program_id(0), pl.program_id(1)
    step = b * nch + c
    total = pl.num_programs(0) * nch
    slot = step % 2
    page_len = cbuf.shape[2]

    def page_copies(st, sl):
        bb, cc = st // nch, st % nch
        out = []
        for p in range(pg):
            page = pt_ref[bb, cc * pg + p]
            out.append(pltpu.make_async_copy(ckv_hbm.at[layer, page], cbuf.at[sl, p], sem.at[0, sl]))
            out.append(pltpu.make_async_copy(kr_hbm.at[layer, page], kbuf.at[sl, :, pl.ds(p * page_len, page_len)],
                                             sem.at[1, sl]))
        return out

    @pl.when(step == 0)
    def _():
        for cp in page_copies(0, 0):
            cp.start()

    @pl.when(step + 1 < total)
    def _():
        for cp in page_copies(step + 1, 1 - slot):
            cp.start()

    @pl.when(c == 0)
    def _():
        m_sc[...] = jnp.full_like(m_sc, NEG_INF)
        l_sc[...] = jnp.zeros_like(l_sc)
        acc_sc[...] = jnp.zeros_like(acc_sc)

    for cp in page_copies(step, slot):
        cp.wait()

    ck = cbuf[slot].reshape(pg * page_len, KV_RANK).astype(BF16)
    q = q_ref[0]
    s = _mm_nt(q[:, 0:KV_RANK], ck) + _mm(q[:, KV_RANK:QK_DIM], kbuf[slot])
    _softmax_update(s, ck, m_sc, l_sc, acc_sc)

    @pl.when(c == nch - 1)
    def _():
        qf = q.astype(F32)
        cn, kn = cnew_ref[0], knew_ref[0]
        s_new = (jnp.sum(qf[:, 0:KV_RANK] * cn, axis=-1, keepdims=True)
                 + jnp.sum(qf[:, KV_RANK:QK_DIM] * kn, axis=-1, keepdims=True))
        m_prev = m_sc[...]
        m_new = jnp.maximum(m_prev, s_new)
        alpha, p_new = jnp.exp(m_prev - m_new), jnp.exp(s_new - m_new)
        l_fin = alpha * l_sc[...] + p_new
        o = (alpha * acc_sc[...] + p_new * cn) / l_fin
        out = jnp.zeros((1, GROUP_W), F32)
        for hh in range(N_HEADS):
            out = out + _mm(o, wuv_ref[hh])[hh:hh + 1, :]
        o_ref[0] = out * gate_ref[0]


def _decode_call(layer, page_table, q16, ckv_new, kr_new, gate, w_uv_pad, cache_ckv, cache_krope):
    nb, n_pages = page_table.shape
    page = cache_ckv.shape[2]
    pg = _tile(n_pages, 64)
    nch = n_pages // pg
    blk = lambda shape: pl.BlockSpec((1,) + shape, lambda b, c, pt: (b, 0, 0))
    grid_spec = pltpu.PrefetchScalarGridSpec(
        num_scalar_prefetch=1, grid=(nb, nch),
        in_specs=[blk((16, QK_W)), blk((1, KV_RANK)), blk((1, ROPE_DIM)), blk((1, GROUP_W)),
                  pl.BlockSpec(w_uv_pad.shape, lambda b, c, pt: (0, 0, 0)),
                  pl.BlockSpec(memory_space=pl.ANY), pl.BlockSpec(memory_space=pl.ANY)],
        out_specs=blk((1, GROUP_W)),
        scratch_shapes=[pltpu.VMEM((2, pg, page, KV_RANK), F32), pltpu.VMEM((2, ROPE_DIM, pg * page), F32),
                        pltpu.SemaphoreType.DMA((2, 2)),
                        pltpu.VMEM((16, 1), F32), pltpu.VMEM((16, 1), F32), pltpu.VMEM((16, KV_RANK), F32)])
    out = pl.pallas_call(
        functools.partial(_decode_body, layer, nch, pg), grid_spec=grid_spec,
        out_shape=jax.ShapeDtypeStruct((nb, 1, GROUP_W), F32),
        compiler_params=_params(("arbitrary", "arbitrary")), name="paged_decode",
    )(page_table, q16, ckv_new[:, None, :], kr_new[:, None, :], gate[:, None, :], w_uv_pad,
      cache_ckv, cache_krope)
    return out.reshape(nb, GROUP_W)


def _memattn_body(q_ref, gate_ref, k_ref, v_ref, o_ref):
    tb = q_ref.shape[0]
    head = lax.broadcasted_iota(jnp.int32, (8, GROUP_W), 1) // HEAD_DIM
    sel = head == lax.broadcasted_iota(jnp.int32, (8, GROUP_W), 0)
    for bi in range(tb):
        q_bd = jnp.where(sel, q_ref[bi], 0.0)
        s = _mm(q_bd, k_ref[0, bi])
        p = jnp.exp(s - jnp.max(s, axis=-1, keepdims=True))
        p = p / jnp.sum(p, axis=-1, keepdims=True)
        o = jnp.where(sel, _mm_nt(p, v_ref[0, bi]), 0.0)
        o_ref[bi] = jnp.sum(o, axis=0, keepdims=True) * gate_ref[bi]


def _memattn_call(layer, mem_q, gate, mem_k, mem_v):
    m = mem_q.shape[0]
    n_mem = mem_k.shape[3]
    tb = _tile(m, 4)
    vec = pl.BlockSpec((tb, 1, GROUP_W), lambda i: (i, 0, 0))
    kv = pl.BlockSpec((1, tb, GROUP_W, n_mem), lambda i: (layer, i, 0, 0))
    out = pl.pallas_call(
        _memattn_body, grid=(m // tb,),
        in_specs=[vec, vec, kv, kv], out_specs=vec,
        out_shape=jax.ShapeDtypeStruct((m, 1, GROUP_W), F32),
        compiler_params=_params(("arbitrary",)), name="mem_attn_decode",
    )(mem_q[:, None, :], gate[:, None, :], mem_k, mem_v)
    return out.reshape(m, GROUP_W)


def _outproj_body(x_ref, m0_ref, m1_ref, m2_ref, m3_ref, w_ref, g_ref, o_ref):
    acc = jnp.zeros(x_ref.shape, F32)
    for i, m_ref in enumerate((m0_ref, m1_ref, m2_ref, m3_ref)):
        acc = acc + _mm(m_ref[...], w_ref[i * GROUP_W:(i + 1) * GROUP_W, :])
    o_ref[...] = x_ref[...] + _rms(acc, g_ref[...])


def _outproj_call(x, mixed, w_out, g):
    m, d = x.shape
    tm = _tile(m, 512)
    row = lambda w: pl.BlockSpec((tm, w), lambda i: (i, 0))
    return pl.pallas_call(
        _outproj_body, grid=(m // tm,),
        in_specs=[row(d)] + [row(GROUP_W)] * 4 + [_const_spec(w_out.shape), _const_spec(g.shape)],
        out_specs=row(d), out_shape=jax.ShapeDtypeStruct((m, d), F32),
        compiler_params=_params(("arbitrary",)), name="out_proj",
    )(x, *mixed, w_out, g)


def _prep_layer(l, pre_norm_g, post_norm_g, w_in, w_out, conv_w, rwkv_mu, rwkv_w0, rwkv_w2, rwkv_a0, rwkv_a2,
                rwkv_k_k, rwkv_k_a, rwkv_r_k, rwkv_lnx_g, rwkv_lnx_b, mla_q_norm_g, mla_w_uq, mla_kv_norm_g,
                mla_w_uk, mla_w_uv, mem_norm_g, w_mem_k, w_mem_v):
    d = w_in.shape[1]
    kr0 = _C_MLAG
    w = w_in[l]
    w_in_p = jnp.concatenate([w[:, :kr0], w[:, kr0 + ROPE_DIM:], w[:, kr0:kr0 + ROPE_DIM],
                              jnp.zeros((d, LANES - ROPE_DIM), F32)], axis=1).astype(BF16)
    uq = mla_w_uq[l].reshape(Q_RANK, N_HEADS, NOPE_DIM + ROPE_DIM)
    w_uq = jnp.concatenate([uq[:, :, :NOPE_DIM].reshape(Q_RANK, -1), uq[:, :, NOPE_DIM:].reshape(Q_RANK, -1)],
                           axis=1).astype(BF16)
    w_uv_pad = jnp.zeros((N_HEADS, KV_RANK, GROUP_W), F32)
    for hh in range(N_HEADS):
        w_uv_pad = w_uv_pad.at[hh, :, hh * HEAD_DIM:(hh + 1) * HEAD_DIM].set(mla_w_uv[l][:, hh, :])
    half = ROPE_DIM // 2
    inv = jnp.power(ROPE_BASE, -jnp.arange(half, dtype=F32) / half)
    zl = jnp.zeros((LORA_W, GROUP_W), F32)
    vecs = jnp.stack([rwkv_w0[l], rwkv_a0[l], rwkv_k_k[l], rwkv_k_a[l], rwkv_r_k[l].reshape(-1),
                      rwkv_lnx_g[l], rwkv_lnx_b[l], jnp.zeros((GROUP_W,), F32)])
    return {
        'pre_g': pre_norm_g[l][None], 'post_g': post_norm_g[l][None], 'w_in': w_in_p,
        'w_out': w_out[l].astype(BF16), 'conv_w': conv_w[l], 'q_norm_g': mla_q_norm_g[l][None], 'w_uq': w_uq,
        'kv_norm_g': mla_kv_norm_g[l][None], 'w_ukT': jnp.transpose(mla_w_uk[l], (1, 2, 0)).astype(BF16),
        'w_uv_pad': w_uv_pad.astype(BF16), 'inv_full': jnp.tile(inv, LANES // half)[None],
        'mu': rwkv_mu[l][None],
        'rwkv_prm': jnp.concatenate([vecs, rwkv_w2[l], zl, zl, rwkv_a2[l]], axis=0),
        'mem_g': mem_norm_g[l][None],
        'w_mem_kv': jnp.concatenate([w_mem_k[l], w_mem_v[l]], axis=1).astype(BF16),
    }


def kernel(x_prompt, x_sample, cache_ckv, cache_krope, cache_mem_k, cache_mem_v, state_conv, state_rwkv_shift, state_rwkv, page_table, mem_prompt, pre_norm_g, post_norm_g, w_in, w_out, conv_w, rwkv_mu, rwkv_w0, rwkv_w2, rwkv_a0, rwkv_a2, rwkv_k_k, rwkv_k_a, rwkv_r_k, rwkv_lnx_g, rwkv_lnx_b, mla_q_norm_g, mla_w_uq, mla_kv_norm_g, mla_w_uk, mla_w_uv, mem_norm_g, w_mem_k, w_mem_v):
    bp, sp, d = x_prompt.shape
    bs, ts, _ = x_sample.shape
    assert ts == 1
    depth = w_in.shape[0]
    n_mem = mem_prompt.shape[1]
    past_len = page_table.shape[1] * cache_ckv.shape[2]
    mem_k4 = jnp.transpose(cache_mem_k, (0, 1, 3, 4, 2)).reshape(depth, bs, GROUP_W, n_mem)
    mem_v4 = jnp.transpose(cache_mem_v, (0, 1, 3, 4, 2)).reshape(depth, bs, GROUP_W, n_mem)
    cache_krope_t = jnp.swapaxes(cache_krope, 2, 3)
    xp = x_prompt.reshape(bp * sp, d)
    xs = x_sample.reshape(bs, d)
    mem2 = mem_prompt.reshape(bp * n_mem, d)
    outs = [[] for _ in range(12)]
    for l in range(depth):
        lw = _prep_layer(l, pre_norm_g, post_norm_g, w_in, w_out, conv_w, rwkv_mu, rwkv_w0, rwkv_w2, rwkv_a0,
                         rwkv_a2, rwkv_k_k, rwkv_k_a, rwkv_r_k, rwkv_lnx_g, rwkv_lnx_b, mla_q_norm_g, mla_w_uq,
                         mla_kv_norm_g, mla_w_uk, mla_w_uv, mem_norm_g, w_mem_k, w_mem_v)
        mk, mv, mk_b, mv_b = _memkv_call(mem2, lw['mem_g'], lw['w_mem_kv'])
        (m_conv, conv_p, r_sh, r_gate, ckv, kr, kcat, qcat, mla_gate, m_mem) = _proj_call(
            xp, lw, decode=False, seq_len=sp,
            mk=mk_b.reshape(bp, n_mem, GROUP_W), mv=mv_b.reshape(bp, n_mem, GROUP_W))
        m_rwkv, st_p = _rwkv_prompt_call(r_sh, r_gate, lw, bp, sp)
        m_mla = _flash_call(qcat, kcat, mla_gate, lw['w_uv_pad'], sp)
        xp = _outproj_call(xp, (m_conv, m_rwkv, m_mla, m_mem), lw['w_out'], lw['post_g'])
        sh_p = r_sh.reshape(bp, sp, RWKV_SHIFT_W)[:, -1]
        (s_conv, cn0, cn1, r_sh_s, r_gate_s, ckv_s, kr_s, _, qcat_s, mla_gate_s, mem_q_s, mem_gate_s) = _proj_call(
            xs, lw, decode=True, seq_len=1, pos0=float(past_len), conv_state=state_conv[l])
        s_rwkv, st_s = _rwkv_step_call(r_sh_s, state_rwkv_shift[l], r_gate_s, state_rwkv[l], lw)
        q16 = jnp.pad(jnp.transpose(qcat_s[0], (1, 0, 2)), ((0, 0), (0, 16 - N_HEADS), (0, 0)))
        s_mla = _decode_call(l, page_table, q16, ckv_s, kr_s, mla_gate_s, lw['w_uv_pad'], cache_ckv, cache_krope_t)
        s_mem = _memattn_call(l, mem_q_s, mem_gate_s, mem_k4, mem_v4)
        xs = _outproj_call(xs, (s_conv, s_rwkv, s_mla, s_mem), lw['w_out'], lw['post_g'])
        vals = (ckv.reshape(bp, sp, KV_RANK), ckv_s.reshape(bs, ts, KV_RANK),
                kr.reshape(bp, sp, ROPE_DIM), kr_s.reshape(bs, ts, ROPE_DIM),
                mk.reshape(bp, n_mem, N_HEADS, HEAD_DIM), mv.reshape(bp, n_mem, N_HEADS, HEAD_DIM),
                conv_p, jnp.stack([cn0, cn1], axis=1), sh_p, r_sh_s, st_p, st_s)
        for o, v in zip(outs, vals):
            o.append(v)
    return (xp.reshape(bp, sp, d), xs.reshape(bs, ts, d)) + tuple(jnp.stack(o) for o in outs)
```

```python
import functools

import jax
import jax.numpy as jnp
import numpy as np
from jax import lax
from jax.experimental import pallas as pl
from jax.experimental.pallas import tpu as pltpu

F32 = jnp.float32
BF16 = jnp.bfloat16

HEAD_DIM = 64
N_HEADS = 4
GROUP_W = N_HEADS * HEAD_DIM
CONV_K = 3
LORA_W = 64
RWKV_SHIFT_W = 3 * GROUP_W + 2 * LORA_W
Q_RANK = 256
KV_RANK = 128
NOPE_DIM = 64
ROPE_DIM = 32
QK_DIM = KV_RANK + ROPE_DIM
QK_W = 2 * KV_RANK
ONES_LANE = QK_DIM
RMS_EPS = 1e-6
LNX_EPS = 64e-5
ROPE_BASE = 10000.0
DECAY_SCALE = 0.6065306597
MLA_SCALE = (NOPE_DIM + ROPE_DIM) ** -0.5
MEM_SCALE = HEAD_DIM ** -0.5
NEG_INF = -1e30

LANES = 128
CHUNK = 64
SUB = 16
VMEM_LIMIT = 56 * 1024 * 1024

_C_CONV, _C_RSH, _C_RGATE, _C_QDOWN, _C_CKV, _C_MLAG, _C_MEMQ, _C_MEMG, _C_KR, _C_END = (
    0, 1024, 1920, 2176, 2432, 2560, 2816, 3072, 3328, 3456)


def _mm(a, b):
    return jnp.dot(a.astype(BF16), b.astype(BF16), preferred_element_type=F32)


def _mm_nt(a, b):
    return lax.dot_general(a.astype(BF16), b.astype(BF16), (((1,), (1,)), ((), ())),
                           preferred_element_type=F32)


def _mm_tn(a, b):
    return lax.dot_general(a.astype(BF16), b.astype(BF16), (((0,), (0,)), ((), ())),
                           preferred_element_type=F32)


def _split_bf16(x, n):
    parts = []
    for _ in range(n):
        p = x.astype(BF16)
        parts.append(p)
        x = x - p.astype(F32)
    return parts


def _mm_hi(a, b):
    (a1, a2), (b1, b2) = _split_bf16(a, 2), _split_bf16(b, 2)
    dot = lambda x, y: jnp.dot(x, y, preferred_element_type=F32)
    return dot(a1, b1) + (dot(a1, b2) + dot(a2, b1))


def _mm_exact_lhs(a01, b):
    a = a01.astype(BF16)
    b1, b2, b3 = _split_bf16(b, 3)
    dot = lambda x: jnp.dot(a, x, preferred_element_type=F32)
    return dot(b1) + (dot(b2) + dot(b3))


def _sigmoid(x):
    return 1.0 / (1.0 + jnp.exp(-x))


def _silu(x):
    return x * _sigmoid(x)


def _rms(x, g):
    return x * lax.rsqrt(jnp.mean(x * x, axis=-1, keepdims=True) + RMS_EPS) * g


def _params(sem):
    return pltpu.CompilerParams(dimension_semantics=sem, vmem_limit_bytes=VMEM_LIMIT)


def _const_spec(shape):
    nd = len(shape)
    return pl.BlockSpec(shape, lambda *_: (0,) * nd)


def _tile(n, pref):
    t = min(n, pref)
    assert n % t == 0, (n, pref)
    return t


def _proj_body(decode, tiles_per_seq, pos0, *refs):
    if decode:
        (x_ref, g_ref, w_ref, cw_ref, qg_ref, wuq_ref, kvg_ref, wuk_ref, inv_ref, cb0_ref, cb1_ref,
         mconv_ref, cn0_ref, cn1_ref, rsh_ref, rgate_ref, ckv_ref, kr_ref, kcat_ref, qcat_ref,
         mlag_ref, memq_ref, memg_ref) = refs
    else:
        (x_ref, g_ref, w_ref, cw_ref, qg_ref, wuq_ref, kvg_ref, wuk_ref, inv_ref, mk_ref, mv_ref,
         mconv_ref, cst_ref, rsh_ref, rgate_ref, ckv_ref, kr_ref, kcat_ref, qcat_ref,
         mlag_ref, mmem_ref, carry_ref, cos_row_ref, sin_row_ref) = refs
    tm = x_ref.shape[0]
    t = pl.program_id(0) % tiles_per_seq
    h = _rms(x_ref[...], g_ref[...]).astype(BF16)

    def proj(lo, hi):
        return jnp.dot(h, w_ref[:, lo:hi], preferred_element_type=F32)

    cv = proj(_C_CONV, _C_RSH)
    c_b, c_c, c_x, c_g = (cv[:, i * GROUP_W:(i + 1) * GROUP_W] for i in range(4))
    u = c_c * c_x
    w0, w1, w2 = cw_ref[0:1, :], cw_ref[1:2, :], cw_ref[2:3, :]
    if decode:
        b0, b1 = cb0_ref[...], cb1_ref[...]
        y = b0 * w0 + b1 * w1 + u * w2
        cn0_ref[...] = b1
        cn1_ref[...] = u
    else:
        @pl.when(t == 0)
        def _():
            carry_ref[...] = jnp.zeros_like(carry_ref)
        p0, p1 = carry_ref[0:1, :], carry_ref[1:2, :]
        row = lax.broadcasted_iota(jnp.int32, u.shape, 0)
        u1 = jnp.where(row == 0, p1, pltpu.roll(u, 1, 0))
        u2 = jnp.where(row == 0, p0, jnp.where(row == 1, p1, pltpu.roll(u, 2, 0)))
        y = u2 * w0 + u1 * w1 + u * w2
        carry_ref[0:2, :] = u[tm - 2:tm, :]
        cst_ref[0] = u[tm - 2:tm, :]
    mconv_ref[...] = (c_b * y * _silu(c_g)).astype(mconv_ref.dtype)

    rsh_ref[...] = proj(_C_RSH, _C_RGATE)
    rgate_ref[...] = _silu(proj(_C_RGATE, _C_QDOWN))

    if decode:
        ang = jnp.full((tm, 1), pos0, F32) * inv_ref[...]
        cos, sin = jnp.cos(ang), jnp.sin(ang)
    else:
        @pl.when(pl.program_id(0) == 0)
        def _():
            ang_row = lax.broadcasted_iota(jnp.int32, (tm, 1), 0).astype(F32) * inv_ref[...]
            cos_row_ref[...] = jnp.cos(ang_row)
            sin_row_ref[...] = jnp.sin(ang_row)
        ang_tile = (t * tm).astype(F32) * inv_ref[...]
        cos_t, sin_t = jnp.cos(ang_tile), jnp.sin(ang_tile)
        cos_r, sin_r = cos_row_ref[...], sin_row_ref[...]
        cos = cos_t * cos_r - sin_t * sin_r
        sin = sin_t * cos_r + cos_t * sin_r
    first = (lax.broadcasted_iota(jnp.int32, (tm, LANES), 1) % ROPE_DIM) < (ROPE_DIM // 2)

    def rope(v):
        partner = jnp.where(first, -pltpu.roll(v, LANES - ROPE_DIM // 2, 1), pltpu.roll(v, ROPE_DIM // 2, 1))
        return v * cos + partner * sin

    qn = _rms(proj(_C_QDOWN, _C_CKV), qg_ref[...])
    q = _mm(qn, wuq_ref[...])
    q_rope = rope(q[:, GROUP_W:GROUP_W + LANES]) * MLA_SCALE
    lane = lax.broadcasted_iota(jnp.int32, (tm, LANES), 1)
    q_lat = _mm(q[:, 0:GROUP_W], wuk_ref[...]) * MLA_SCALE
    for hh in range(N_HEADS):
        qcat_ref[0, hh, :, 0:KV_RANK] = q_lat[:, hh * KV_RANK:(hh + 1) * KV_RANK].astype(qcat_ref.dtype)
        q_r = q_rope if hh == 0 else pltpu.roll(q_rope, LANES - hh * ROPE_DIM, 1)
        qcat_ref[0, hh, :, KV_RANK:QK_W] = jnp.where(lane < ROPE_DIM, q_r, 0.0).astype(qcat_ref.dtype)
    ckv = _rms(proj(_C_CKV, _C_MLAG), kvg_ref[...])
    krope = rope(proj(_C_KR, _C_END))
    ckv_ref[...] = ckv
    kr_ref[...] = krope[:, 0:ROPE_DIM]
    kcat_ref[:, 0:KV_RANK] = ckv.astype(kcat_ref.dtype)
    kcat_ref[:, KV_RANK:QK_W] = jnp.where(lane == ONES_LANE - KV_RANK, 1.0, krope).astype(kcat_ref.dtype)
    mlag_ref[...] = _silu(proj(_C_MLAG, _C_MEMQ))

    mq = proj(_C_MEMQ, _C_MEMG) * MEM_SCALE
    mg = _silu(proj(_C_MEMG, _C_KR))
    if decode:
        memq_ref[...] = mq
        memg_ref[...] = mg
    else:
        mk, mv = mk_ref[0], mv_ref[0]
        head = lax.broadcasted_iota(jnp.int32, (tm, GROUP_W), 1) // HEAD_DIM
        acc = jnp.zeros((tm, GROUP_W), F32)
        for hh in range(N_HEADS):
            s = _mm_nt(jnp.where(head == hh, mq, 0.0), mk)
            p = jnp.exp(s - jnp.max(s, axis=-1, keepdims=True))
            p = p / jnp.sum(p, axis=-1, keepdims=True)
            acc = acc + jnp.where(head == hh, _mm(p, mv), 0.0)
        mmem_ref[...] = (acc * mg).astype(mmem_ref.dtype)


def _proj_call(x, lw, *, decode, seq_len, pos0=0.0, conv_state=None, mk=None, mv=None):
    m, d = x.shape
    tm = _tile(seq_len if not decode else m, 512)
    tps = 1 if decode else seq_len // tm
    nb = m // (tm * tps)
    grid = (m // tm,)
    row = lambda w: pl.BlockSpec((tm, w), lambda i: (i, 0))
    consts = [lw['pre_g'], lw['w_in'], lw['conv_w'], lw['q_norm_g'], lw['w_uq'], lw['kv_norm_g'],
              lw['w_ukT'], lw['inv_full']]
    in_specs = [row(d)] + [_const_spec(c.shape) for c in consts]
    qcat_shape = (nb, N_HEADS, tm * tps, QK_W)
    qcat_spec = pl.BlockSpec((1, N_HEADS, tm, QK_W), lambda i: (i // tps, 0, i % tps, 0))
    sds = jax.ShapeDtypeStruct
    common_out = [
        (sds((m, RWKV_SHIFT_W), F32), row(RWKV_SHIFT_W)),
        (sds((m, GROUP_W), F32), row(GROUP_W)),
        (sds((m, KV_RANK), F32), row(KV_RANK)),
        (sds((m, ROPE_DIM), F32), row(ROPE_DIM)),
        (sds((m, QK_W), BF16), row(QK_W)),
        (sds(qcat_shape, BF16), qcat_spec),
        (sds((m, GROUP_W), F32), row(GROUP_W)),
    ]
    if decode:
        ins = [x] + consts + [conv_state[:, 0], conv_state[:, 1]]
        in_specs += [row(GROUP_W), row(GROUP_W)]
        outs = ([(sds((m, GROUP_W), BF16), row(GROUP_W)),
                 (sds((m, GROUP_W), F32), row(GROUP_W)), (sds((m, GROUP_W), F32), row(GROUP_W))]
                + common_out
                + [(sds((m, GROUP_W), F32), row(GROUP_W)), (sds((m, GROUP_W), F32), row(GROUP_W))])
        scratch = []
    else:
        ins = [x] + consts + [mk, mv]
        n_mem = mk.shape[1]
        in_specs += [pl.BlockSpec((1, n_mem, GROUP_W), lambda i: (i // tps, 0, 0))] * 2
        outs = ([(sds((m, GROUP_W), BF16), row(GROUP_W)),
                 (sds((nb, CONV_K - 1, GROUP_W), F32),
                  pl.BlockSpec((1, CONV_K - 1, GROUP_W), lambda i: (i // tps, 0, 0)))]
                + common_out
                + [(sds((m, GROUP_W), BF16), row(GROUP_W))])
        scratch = [pltpu.VMEM((8, GROUP_W), F32), pltpu.VMEM((tm, LANES), F32), pltpu.VMEM((tm, LANES), F32)]
    return pl.pallas_call(
        functools.partial(_proj_body, decode, tps, pos0),
        grid=grid, in_specs=in_specs,
        out_specs=[o[1] for o in outs], out_shape=[o[0] for o in outs],
        scratch_shapes=scratch, compiler_params=_params(("arbitrary",)),
        name="proj_decode" if decode else "proj_prompt",
    )(*ins)


def _memkv_body(x_ref, g_ref, w_ref, k_ref, v_ref, kb_ref, vb_ref):
    kv = _mm(_rms(x_ref[...], g_ref[...]), w_ref[...])
    k, v = kv[:, :GROUP_W], kv[:, GROUP_W:]
    k_ref[...] = k
    v_ref[...] = v
    kb_ref[...] = k.astype(BF16)
    vb_ref[...] = v.astype(BF16)


def _memkv_call(mem, g, w_kv):
    m, d = mem.shape
    tm = _tile(m, 512)
    row = lambda w: pl.BlockSpec((tm, w), lambda i: (i, 0))
    sds = jax.ShapeDtypeStruct
    return pl.pallas_call(
        _memkv_body, grid=(m // tm,),
        in_specs=[row(d), _const_spec(g.shape), _const_spec(w_kv.shape)],
        out_specs=[row(GROUP_W)] * 4,
        out_shape=[sds((m, GROUP_W), F32), sds((m, GROUP_W), F32), sds((m, GROUP_W), BF16), sds((m, GROUP_W), BF16)],
        compiler_params=_params(("arbitrary",)), name="mem_kv",
    )(mem, g, w_kv)


def _head_ones():
    r = lax.broadcasted_iota(jnp.int32, (GROUP_W, GROUP_W), 0) // HEAD_DIM
    c = lax.broadcasted_iota(jnp.int32, (GROUP_W, GROUP_W), 1) // HEAD_DIM
    return (r == c).astype(F32)


def _rwkv_tokens(mixed, prm, ones_bd):
    w0, a0, k_k, k_a, r_k = (prm[i:i + 1, :] for i in range(5))
    w2p, a2p = prm[8:8 + LANES, :], prm[8 + LANES:8 + 2 * LANES, :]
    r, k, v = mixed[:, 0:GROUP_W], mixed[:, GROUP_W:2 * GROUP_W], mixed[:, 2 * GROUP_W:3 * GROUP_W]
    wa = mixed[:, 3 * GROUP_W:RWKV_SHIFT_W]
    logw = -DECAY_SCALE * _sigmoid(w0 + _mm_hi(jnp.tanh(wa), w2p))
    a = _sigmoid(a0 + _mm_hi(wa, a2p))
    kkr = k * k_k
    kk = kkr * lax.rsqrt(_mm(kkr * kkr, ones_bd) + 1e-12)
    k2 = k * (1.0 + (a - 1.0) * k_a)
    bonus = _mm(r * k2 * r_k, ones_bd) * v
    return r, k2, v, logw, kk, a, bonus


def _rwkv_finish(y, bonus, gate, prm, ones_bd):
    lnx_g, lnx_b = prm[5:6, :], prm[6:7, :]
    mean = _mm(y, ones_bd) * (1.0 / HEAD_DIM)
    yc = y - mean
    var = _mm(yc * yc, ones_bd) * (1.0 / HEAD_DIM)
    return ((yc * lax.rsqrt(var + LNX_EPS)) * lnx_g + lnx_b + bonus) * gate


def _stack_heads(x, head_lane):
    return jnp.concatenate([jnp.where(head_lane == hh, x, 0.0) for hh in range(N_HEADS)], axis=0)


def _unstack_heads(x):
    c = x.shape[0] // N_HEADS
    return x[0:c] + x[c:2 * c] + x[2 * c:3 * c] + x[3 * c:4 * c]


def _rwkv_prompt_body(x_ref, gate_ref, mu_ref, prm_ref, o_ref, st_ref, prev_ref, s_ref):
    ct = x_ref.shape[1]
    si = pl.program_id(1)

    @pl.when(si == 0)
    def _():
        prev_ref[...] = jnp.zeros_like(prev_ref)
        s_ref[...] = jnp.zeros_like(s_ref)

    x = x_ref[0]
    row = lax.broadcasted_iota(jnp.int32, x.shape, 0)
    prev = jnp.where(row == 0, prev_ref[0:1, :], pltpu.roll(x, 1, 0))
    prev_ref[0:1, :] = x[ct - 1:ct, :]
    mixed = x + (prev - x) * mu_ref[...]
    prm = prm_ref[...]
    ones_bd = _head_ones()
    r, k2, v, logw, kk, a, bonus = _rwkv_tokens(mixed, prm, ones_bd)
    b = kk * a

    n = N_HEADS * CHUNK
    head_lane = lax.broadcasted_iota(jnp.int32, (CHUNK, GROUP_W), 1) // HEAD_DIM
    rr = lax.broadcasted_iota(jnp.int32, (n, n), 0)
    cc = lax.broadcasted_iota(jnp.int32, (n, n), 1)
    same_sub = (rr // SUB) == (cc // SUB)
    same_head = (rr // CHUNK) == (cc // CHUNK)
    eye = (rr == cc).astype(F32)
    t_w = lax.broadcasted_iota(jnp.int32, (CHUNK, GROUP_W), 0)
    s_w = lax.broadcasted_iota(jnp.int32, (CHUNK, GROUP_W), 1) % CHUNK
    strict_w, incl_w = t_w > s_w, t_w >= s_w
    eye_w = (t_w == s_w).astype(F32)
    eye_sub = (lax.broadcasted_iota(jnp.int32, (SUB, GROUP_W), 0)
               == lax.broadcasted_iota(jnp.int32, (SUB, GROUP_W), 1) % SUB).astype(F32)
    tri = (lax.broadcasted_iota(jnp.int32, (CHUNK, CHUNK), 0)
           >= lax.broadcasted_iota(jnp.int32, (CHUNK, CHUNK), 1)).astype(F32)

    def sub_square(z):
        return jnp.where(same_sub, jnp.concatenate([z] * (n // SUB), axis=0), 0.0)

    def sub_strip(z):
        out = z[0:SUB]
        for i in range(1, n // SUB):
            out = out + z[i * SUB:(i + 1) * SUB]
        return out

    chunks = range(ct // CHUNK)
    each = lambda f, *xs: [f(*(x[c] for x in xs)) for c in chunks]
    rows = lambda z: [z[c * CHUNK:(c + 1) * CHUNK] for c in chunks]
    stack = lambda z: _stack_heads(z, head_lane)
    lw = rows(logw)
    cw = each(lambda l: _mm_exact_lhs(tri, l), lw)
    cwl = each(lambda z: z[CHUNK - 1:CHUNK, :], cw)
    e_ng = each(lambda z: jnp.exp(-z), cw)
    e_w = each(lambda z, zl: jnp.exp(zl - z), cw, cwl)
    kkd = each(lambda x, z, l: x * jnp.exp(z - l), rows(kk), cw, lw)
    rd = each(lambda x, z: x * jnp.exp(z), rows(r), cw)
    kinvw = each(lambda x, e: x * e, rows(k2), e_w)
    binvw = each(lambda x, e: x * e, rows(b), e_w)
    v_w = rows(v)
    kkd_sq = each(stack, kkd)
    kinv_sq = each(lambda x, e: stack(x * e), rows(k2), e_ng)
    binv_sq = each(lambda x, e: stack(x * e), rows(b), e_ng)
    v_sq = each(stack, v_w)
    kr_rows = each(lambda x, y: jnp.concatenate([x, y], axis=0), kkd, rd)
    ab = each(_mm_nt, kr_rows, binv_sq)
    ak = each(_mm_nt, kr_rows, kinv_sq)
    a_bk = each(lambda z: jnp.where(strict_w, z[0:CHUNK], 0.0), ab)
    m_rb = each(lambda z: jnp.where(incl_w, z[CHUNK:2 * CHUNK], 0.0), ab)
    a_vk = each(lambda z: jnp.where(strict_w, z[0:CHUNK], 0.0), ak)
    m_rk = each(lambda z: jnp.where(incl_w, z[CHUNK:2 * CHUNK], 0.0), ak)
    a_sq = each(stack, a_bk)
    d1_sq = each(lambda x: jnp.where(same_sub, x, 0.0), a_sq)
    e1_sq = each(lambda x, y: x - y, a_sq, d1_sq)
    d1 = each(sub_strip, d1_sq)
    d2 = each(_mm, d1, d1_sq)
    t0 = each(lambda x: eye_sub - x, d1)
    r2 = each(lambda x, y, z: _mm(jnp.concatenate([x, y], axis=0), sub_square(z)), d2, t0, d2)
    d4 = each(lambda z: z[0:SUB], r2)
    t1 = each(lambda x, z: x + z[SUB:2 * SUB], t0, r2)
    r4 = each(lambda x, y, z: _mm(jnp.concatenate([x, y], axis=0), sub_square(z)), d4, t1, d4)
    d8 = each(lambda z: z[0:SUB], r4)
    t2 = each(lambda x, z: x + z[SUB:2 * SUB], t1, r4)
    dinv = each(lambda x, y: x + _mm(x, sub_square(y)), t2, d8)
    dinv_sq = each(sub_square, dinv)
    dinv_w = each(_unstack_heads, dinv_sq)
    nn = each(_mm, dinv_w, e1_sq)
    nn_sq = each(stack, nn)
    n2 = each(_mm, nn, nn_sq)
    t3 = each(lambda x, y: _mm(eye_w - x, eye + stack(y)), nn, n2)
    tinv = each(_mm, t3, dinv_sq)
    xv = each(lambda x, y, z: _mm(jnp.concatenate([x, y], axis=0), z), a_vk, m_rk, v_sq)
    x_w = each(lambda z: z[0:CHUNK], xv)
    kkdp = each(_mm, tinv, kkd_sq)
    vp = each(_mm, tinv, each(stack, x_w))
    kkdp_sq = each(stack, kkdp)
    rq = each(lambda x, y, z: x - _mm(y, z), rd, m_rb, kkdp_sq)
    y_in = each(lambda z, a3, a4: z[CHUNK:2 * CHUNK] - _mm(a3, stack(a4)), xv, m_rb, vp)
    phi = each(lambda zl, x, y: eye * jnp.exp(zl) - jnp.where(same_head, _mm_tn(x, y), 0.0), cwl, kkdp, binvw)
    g = each(lambda a1, a2, a3, a4: _unstack_heads(jnp.where(
        same_head, _mm_tn(jnp.concatenate([a1, -a3], axis=0), jnp.concatenate([a2, a4], axis=0)), 0.0)),
        v_w, kinvw, vp, binvw)
    state = s_ref[...]
    ys = []
    for c in chunks:
        ys.append(_mm_nt(rq[c], stack(state)) + y_in[c])
        state = _mm(state, phi[c]) + g[c]
    s_ref[...] = state
    st_ref[0] = state
    y = jnp.concatenate(ys, axis=0) if len(ys) > 1 else ys[0]
    o_ref[...] = _rwkv_finish(y, bonus, gate_ref[...], prm, ones_bd).astype(o_ref.dtype)


def _rwkv_prompt_call(r_sh, gate, lw, nb, seq_len):
    ct = _tile(seq_len, 512)
    ns = seq_len // ct
    x3 = r_sh.reshape(nb, seq_len, RWKV_SHIFT_W)
    sds = jax.ShapeDtypeStruct
    out, st = pl.pallas_call(
        _rwkv_prompt_body, grid=(nb, ns),
        in_specs=[pl.BlockSpec((1, ct, RWKV_SHIFT_W), lambda b, s: (b, s, 0)),
                  pl.BlockSpec((ct, GROUP_W), lambda b, s: (b * ns + s, 0)),
                  _const_spec(lw['mu'].shape), _const_spec(lw['rwkv_prm'].shape)],
        out_specs=[pl.BlockSpec((ct, GROUP_W), lambda b, s: (b * ns + s, 0)),
                   pl.BlockSpec((1, HEAD_DIM, GROUP_W), lambda b, s: (b, 0, 0))],
        out_shape=[sds((nb * seq_len, GROUP_W), BF16), sds((nb, HEAD_DIM, GROUP_W), F32)],
        scratch_shapes=[pltpu.VMEM((8, RWKV_SHIFT_W), F32),
                        pltpu.VMEM((HEAD_DIM, GROUP_W), F32)],
        compiler_params=_params(("arbitrary", "arbitrary")), name="rwkv_prompt",
    )(x3, gate, lw['mu'], lw['rwkv_prm'])
    st = st.reshape(nb, HEAD_DIM, N_HEADS, HEAD_DIM).transpose(0, 2, 1, 3)
    return out, st


def _rwkv_step_body(x_ref, prev_ref, gate_ref, mu_ref, prm_ref, s_ref, o_ref, so_ref, y_ref):
    tb = x_ref.shape[0]
    x = x_ref[...]
    mixed = x + (prev_ref[...] - x) * mu_ref[...]
    prm = prm_ref[...]
    ones_bd = _head_ones()
    r, k2, v, logw, kk, a, bonus = _rwkv_tokens(mixed, prm, ones_bd)
    w = jnp.exp(logw)
    b = kk * a
    eye = (lax.broadcasted_iota(jnp.int32, (HEAD_DIM, HEAD_DIM), 0)
           == lax.broadcasted_iota(jnp.int32, (HEAD_DIM, HEAD_DIM), 1))
    for bi in range(tb):
        for hh in range(N_HEADS):
            hs = slice(hh * HEAD_DIM, (hh + 1) * HEAD_DIM)
            rowv = lambda z: z[bi:bi + 1, hs]
            st = s_ref[bi, hh]
            sa = jnp.sum(st * rowv(kk), axis=1, keepdims=True)
            v_col = jnp.sum(jnp.where(eye, rowv(v), 0.0), axis=1, keepdims=True)
            st = st * rowv(w) - sa * rowv(b) + v_col * rowv(k2)
            so_ref[bi, hh] = st
            y_col = jnp.sum(st * rowv(r), axis=1, keepdims=True)
            y_ref[bi:bi + 1, hs] = jnp.sum(jnp.where(eye, y_col, 0.0), axis=0, keepdims=True)
    o_ref[...] = _rwkv_finish(y_ref[...], bonus, gate_ref[...], prm, ones_bd).astype(o_ref.dtype)


def _rwkv_step_call(r_sh, prev, gate, state, lw):
    m = r_sh.shape[0]
    tb = _tile(m, 8)
    row = lambda w: pl.BlockSpec((tb, w), lambda i: (i, 0))
    st_spec = pl.BlockSpec((tb, N_HEADS, HEAD_DIM, HEAD_DIM), lambda i: (i, 0, 0, 0))
    sds = jax.ShapeDtypeStruct
    return pl.pallas_call(
        _rwkv_step_body, grid=(m // tb,),
        in_specs=[row(RWKV_SHIFT_W), row(RWKV_SHIFT_W), row(GROUP_W),
                  _const_spec(lw['mu'].shape), _const_spec(lw['rwkv_prm'].shape), st_spec],
        out_specs=[row(GROUP_W), st_spec],
        out_shape=[sds((m, GROUP_W), BF16), sds(state.shape, F32)],
        scratch_shapes=[pltpu.VMEM((tb, GROUP_W), F32)],
        compiler_params=_params(("arbitrary",)), name="rwkv_step",
    )(r_sh, prev, gate, lw['mu'], lw['rwkv_prm'], state)


def _softmax_update(s, v, m_sc, l_sc, acc_sc):
    m_prev = m_sc[...]
    m_new = jnp.maximum(m_prev, jnp.max(s, axis=-1, keepdims=True))
    alpha = jnp.exp(m_prev - m_new)
    p = jnp.exp(s - m_new)
    l_sc[...] = alpha * l_sc[...] + jnp.sum(p, axis=-1, keepdims=True)
    acc_sc[...] = alpha * acc_sc[...] + _mm(p, v)
    m_sc[...] = m_new


def _flash_body(q_ref, k_ref, gate_ref, wuv_ref, x_ref, mconv_ref, mrwkv_ref, mmem_ref, wout_ref, pg_ref,
                o_ref, sa_ref, sb_ref, m_ref, acc_ref):
    qi = pl.program_id(1)
    tq = q_ref.shape[2]

    def keys(kb):
        return k_ref[0, pl.ds(pl.multiple_of(kb * tq, tq), tq), :]

    def scores(kb, s_ref):
        k = keys(kb)
        for hh in range(N_HEADS):
            s_ref[hh] = _mm_nt(q_ref[0, hh], k)

    def update(s_ref, kb, masked):
        k = keys(kb)
        for hh in range(N_HEADS):
            s_h = s_ref[hh]
            if masked:
                s_h = jnp.where(lax.broadcasted_iota(jnp.int32, s_h.shape, 1)
                                <= lax.broadcasted_iota(jnp.int32, s_h.shape, 0), s_h, NEG_INF)
            m_prev = m_ref[hh]
            m_new = jnp.maximum(m_prev, jnp.max(s_h, axis=-1, keepdims=True))
            p = jnp.exp(s_h - m_new)
            acc_ref[hh] = jnp.exp(m_prev - m_new) * acc_ref[hh] + _mm(p, k)
            m_ref[hh] = m_new

    def finish(s_ref):
        update(s_ref, qi, True)
        mla = jnp.zeros((tq, GROUP_W), F32)
        for hh in range(N_HEADS):
            acc = acc_ref[hh]
            o = acc[:, 0:KV_RANK] / acc[:, ONES_LANE:ONES_LANE + 1]
            mla = mla + _mm(o, wuv_ref[hh])
        mixed = (mconv_ref[...], mrwkv_ref[...], (mla * gate_ref[...]).astype(BF16), mmem_ref[...])
        y = jnp.zeros(x_ref.shape, F32)
        for i, m_i in enumerate(mixed):
            y = y + _mm(m_i, wout_ref[i * GROUP_W:(i + 1) * GROUP_W, :])
        o_ref[...] = x_ref[...] + _rms(y, pg_ref[...])

    m_ref[...] = jnp.full_like(m_ref, NEG_INF)
    acc_ref[...] = jnp.zeros_like(acc_ref)
    scores(0, sa_ref)

    def pair(j, _):
        kb = 2 * j
        scores(kb + 1, sb_ref)
        update(sa_ref, kb, False)
        scores(kb + 2, sa_ref)
        update(sb_ref, kb + 1, False)
        return 0

    lax.fori_loop(0, qi // 2, pair, 0)

    @pl.when(qi % 2 == 1)
    def _():
        scores(qi, sb_ref)
        update(sa_ref, qi - 1, False)
        finish(sb_ref)

    @pl.when(qi % 2 == 0)
    def _():
        finish(sa_ref)


def _flash_call(qcat, kcat, gate, w_uv_pad, x, m_conv, m_rwkv, m_mem, w_out, post_g, seq_len):
    nb = qcat.shape[0]
    d = x.shape[1]
    tq = _tile(seq_len, 512)
    nq = seq_len // tq
    k3 = kcat.reshape(nb, seq_len, QK_W)
    row = lambda w: pl.BlockSpec((tq, w), lambda b, i: (b * nq + i, 0))
    return pl.pallas_call(
        _flash_body, grid=(nb, nq),
        in_specs=[pl.BlockSpec((1, N_HEADS, tq, QK_W), lambda b, i: (b, 0, i, 0)),
                  pl.BlockSpec((1, seq_len, QK_W), lambda b, i: (b, 0, 0)),
                  row(GROUP_W), _const_spec(w_uv_pad.shape),
                  row(d), row(GROUP_W), row(GROUP_W), row(GROUP_W),
                  _const_spec(w_out.shape), _const_spec(post_g.shape)],
        out_specs=row(d),
        out_shape=jax.ShapeDtypeStruct((nb * seq_len, d), F32),
        scratch_shapes=[pltpu.VMEM((N_HEADS, tq, tq), F32), pltpu.VMEM((N_HEADS, tq, tq), F32),
                        pltpu.VMEM((N_HEADS, tq, 1), F32), pltpu.VMEM((N_HEADS, tq, QK_W), F32)],
        compiler_params=_params(("arbitrary", "arbitrary")), name="flash_mla",
    )(qcat, k3, gate, w_uv_pad, x, m_conv, m_rwkv, m_mem, w_out, post_g)


def _decode_body(layer, nch, pg, pt_ref, q_ref, cnew_ref, knew_ref, gate_ref, wuv_ref, ckv_hbm, kr_hbm,
                 o_ref, cbuf, kbuf, sem, m_sc, l_sc, acc_sc):
    b, c = pl.program_id(0), pl.program_id(1)
    step = b * nch + c
    total = pl.num_programs(0) * nch
    slot = step % 2
    page_len = cbuf.shape[2]

    def page_copies(st, sl):
        bb, cc = st // nch, st % nch
        out = []
        for p in range(pg):
            page = pt_ref[bb, cc * pg + p]
            out.append(pltpu.make_async_copy(ckv_hbm.at[layer, page], cbuf.at[sl, p], sem.at[0, sl]))
            out.append(pltpu.make_async_copy(kr_hbm.at[layer, page], kbuf.at[sl, :, pl.ds(p * page_len, page_len)],
                                             sem.at[1, sl]))
        return out

    @pl.when(step == 0)
    def _():
        for cp in page_copies(0, 0):
            cp.start()

    @pl.when(step + 1 < total)
    def _():
        for cp in page_copies(step + 1, 1 - slot):
            cp.start()

    @pl.when(c == 0)
    def _():
        m_sc[...] = jnp.full_like(m_sc, NEG_INF)
        l_sc[...] = jnp.zeros_like(l_sc)
        acc_sc[...] = jnp.zeros_like(acc_sc)

    for cp in page_copies(step, slot):
        cp.wait()

    ck = cbuf[slot].reshape(pg * page_len, KV_RANK).astype(BF16)
    q = q_ref[0]
    s = _mm_nt(q[:, 0:KV_RANK], ck) + _mm(q[:, KV_RANK:QK_DIM], kbuf[slot])
    _softmax_update(s, ck, m_sc, l_sc, acc_sc)

    @pl.when(c == nch - 1)
    def _():
        qf = q.astype(F32)
        cn, kn = cnew_ref[0], knew_ref[0]
        s_new = (jnp.sum(qf[:, 0:KV_RANK] * cn, axis=-1, keepdims=True)
                 + jnp.sum(qf[:, KV_RANK:QK_DIM] * kn, axis=-1, keepdims=True))
        m_prev = m_sc[...]
        m_new = jnp.maximum(m_prev, s_new)
        alpha, p_new = jnp.exp(m_prev - m_new), jnp.exp(s_new - m_new)
        l_fin = alpha * l_sc[...] + p_new
        o = (alpha * acc_sc[...] + p_new * cn) / l_fin
        out = jnp.zeros((1, GROUP_W), F32)
        for hh in range(N_HEADS):
            out = out + _mm(o, wuv_ref[hh])[hh:hh + 1, :]
        o_ref[0] = out * gate_ref[0]


def _decode_call(layer, page_table, q16, ckv_new, kr_new, gate, w_uv_pad, cache_ckv, cache_krope):
    nb, n_pages = page_table.shape
    page = cache_ckv.shape[2]
    pg = _tile(n_pages, 64)
    nch = n_pages // pg
    blk = lambda shape: pl.BlockSpec((1,) + shape, lambda b, c, pt: (b, 0, 0))
    grid_spec = pltpu.PrefetchScalarGridSpec(
        num_scalar_prefetch=1, grid=(nb, nch),
        in_specs=[blk((16, QK_W)), blk((1, KV_RANK)), blk((1, ROPE_DIM)), blk((1, GROUP_W)),
                  pl.BlockSpec(w_uv_pad.shape, lambda b, c, pt: (0, 0, 0)),
                  pl.BlockSpec(memory_space=pl.ANY), pl.BlockSpec(memory_space=pl.ANY)],
        out_specs=blk((1, GROUP_W)),
        scratch_shapes=[pltpu.VMEM((2, pg, page, KV_RANK), F32), pltpu.VMEM((2, ROPE_DIM, pg * page), F32),
                        pltpu.SemaphoreType.DMA((2, 2)),
                        pltpu.VMEM((16, 1), F32), pltpu.VMEM((16, 1), F32), pltpu.VMEM((16, KV_RANK), F32)])
    out = pl.pallas_call(
        functools.partial(_decode_body, layer, nch, pg), grid_spec=grid_spec,
        out_shape=jax.ShapeDtypeStruct((nb, 1, GROUP_W), F32),
        compiler_params=_params(("arbitrary", "arbitrary")), name="paged_decode",
    )(page_table, q16, ckv_new[:, None, :], kr_new[:, None, :], gate[:, None, :], w_uv_pad,
      cache_ckv, cache_krope)
    return out.reshape(nb, GROUP_W)


def _memattn_body(q_ref, gate_ref, k_ref, v_ref, o_ref):
    tb = q_ref.shape[0]
    head = lax.broadcasted_iota(jnp.int32, (8, GROUP_W), 1) // HEAD_DIM
    sel = head == lax.broadcasted_iota(jnp.int32, (8, GROUP_W), 0)
    for bi in range(tb):
        q_bd = jnp.where(sel, q_ref[bi], 0.0)
        s = _mm(q_bd, k_ref[0, bi])
        p = jnp.exp(s - jnp.max(s, axis=-1, keepdims=True))
        p = p / jnp.sum(p, axis=-1, keepdims=True)
        o = jnp.where(sel, _mm_nt(p, v_ref[0, bi]), 0.0)
        o_ref[bi] = jnp.sum(o, axis=0, keepdims=True) * gate_ref[bi]


def _memattn_call(layer, mem_q, gate, mem_k, mem_v):
    m = mem_q.shape[0]
    n_mem = mem_k.shape[3]
    tb = _tile(m, 4)
    vec = pl.BlockSpec((tb, 1, GROUP_W), lambda i: (i, 0, 0))
    kv = pl.BlockSpec((1, tb, GROUP_W, n_mem), lambda i: (layer, i, 0, 0))
    out = pl.pallas_call(
        _memattn_body, grid=(m // tb,),
        in_specs=[vec, vec, kv, kv], out_specs=vec,
        out_shape=jax.ShapeDtypeStruct((m, 1, GROUP_W), F32),
        compiler_params=_params(("arbitrary",)), name="mem_attn_decode",
    )(mem_q[:, None, :], gate[:, None, :], mem_k, mem_v)
    return out.reshape(m, GROUP_W)


def _outproj_body(x_ref, m0_ref, m1_ref, m2_ref, m3_ref, w_ref, g_ref, o_ref):
    acc = jnp.zeros(x_ref.shape, F32)
    for i, m_ref in enumerate((m0_ref, m1_ref, m2_ref, m3_ref)):
        acc = acc + _mm(m_ref[...], w_ref[i * GROUP_W:(i + 1) * GROUP_W, :])
    o_ref[...] = x_ref[...] + _rms(acc, g_ref[...])


def _outproj_call(x, mixed, w_out, g):
    m, d = x.shape
    tm = _tile(m, 512)
    row = lambda w: pl.BlockSpec((tm, w), lambda i: (i, 0))
    return pl.pallas_call(
        _outproj_body, grid=(m // tm,),
        in_specs=[row(d)] + [row(GROUP_W)] * 4 + [_const_spec(w_out.shape), _const_spec(g.shape)],
        out_specs=row(d), out_shape=jax.ShapeDtypeStruct((m, d), F32),
        compiler_params=_params(("arbitrary",)), name="out_proj",
    )(x, *mixed, w_out, g)


def _prep_layer(l, pre_norm_g, post_norm_g, w_in, w_out, conv_w, rwkv_mu, rwkv_w0, rwkv_w2, rwkv_a0, rwkv_a2,
                rwkv_k_k, rwkv_k_a, rwkv_r_k, rwkv_lnx_g, rwkv_lnx_b, mla_q_norm_g, mla_w_uq, mla_kv_norm_g,
                mla_w_uk, mla_w_uv, mem_norm_g, w_mem_k, w_mem_v):
    d = w_in.shape[1]
    kr0 = _C_MLAG
    w = w_in[l]
    w_in_p = jnp.concatenate([w[:, :kr0], w[:, kr0 + ROPE_DIM:], w[:, kr0:kr0 + ROPE_DIM],
                              jnp.zeros((d, LANES - ROPE_DIM), F32)], axis=1).astype(BF16)
    uq = mla_w_uq[l].reshape(Q_RANK, N_HEADS, NOPE_DIM + ROPE_DIM)
    w_uq = jnp.concatenate([uq[:, :, :NOPE_DIM].reshape(Q_RANK, -1), uq[:, :, NOPE_DIM:].reshape(Q_RANK, -1)],
                           axis=1).astype(BF16)
    w_uv_pad = jnp.zeros((N_HEADS, KV_RANK, GROUP_W), F32)
    for hh in range(N_HEADS):
        w_uv_pad = w_uv_pad.at[hh, :, hh * HEAD_DIM:(hh + 1) * HEAD_DIM].set(mla_w_uv[l][:, hh, :])
    w_uk_bd = jnp.zeros((N_HEADS * NOPE_DIM, N_HEADS * KV_RANK), F32)
    for hh in range(N_HEADS):
        w_uk_bd = w_uk_bd.at[hh * NOPE_DIM:(hh + 1) * NOPE_DIM, hh * KV_RANK:(hh + 1) * KV_RANK].set(
            mla_w_uk[l][:, hh, :].T)
    half = ROPE_DIM // 2
    inv = jnp.power(ROPE_BASE, -jnp.arange(half, dtype=F32) / half)
    zl = jnp.zeros((LORA_W, GROUP_W), F32)
    vecs = jnp.stack([rwkv_w0[l], rwkv_a0[l], rwkv_k_k[l], rwkv_k_a[l], rwkv_r_k[l].reshape(-1),
                      rwkv_lnx_g[l], rwkv_lnx_b[l], jnp.zeros((GROUP_W,), F32)])
    return {
        'pre_g': pre_norm_g[l][None], 'post_g': post_norm_g[l][None], 'w_in': w_in_p,
        'w_out': w_out[l].astype(BF16), 'conv_w': conv_w[l], 'q_norm_g': mla_q_norm_g[l][None], 'w_uq': w_uq,
        'kv_norm_g': mla_kv_norm_g[l][None], 'w_ukT': w_uk_bd.astype(BF16),
        'w_uv_pad': w_uv_pad.astype(BF16), 'inv_full': jnp.tile(inv, LANES // half)[None],
        'mu': rwkv_mu[l][None],
        'rwkv_prm': jnp.concatenate([vecs, rwkv_w2[l], zl, zl, rwkv_a2[l]], axis=0),
        'mem_g': mem_norm_g[l][None],
        'w_mem_kv': jnp.concatenate([w_mem_k[l], w_mem_v[l]], axis=1).astype(BF16),
    }


def kernel(x_prompt, x_sample, cache_ckv, cache_krope, cache_mem_k, cache_mem_v, state_conv, state_rwkv_shift, state_rwkv, page_table, mem_prompt, pre_norm_g, post_norm_g, w_in, w_out, conv_w, rwkv_mu, rwkv_w0, rwkv_w2, rwkv_a0, rwkv_a2, rwkv_k_k, rwkv_k_a, rwkv_r_k, rwkv_lnx_g, rwkv_lnx_b, mla_q_norm_g, mla_w_uq, mla_kv_norm_g, mla_w_uk, mla_w_uv, mem_norm_g, w_mem_k, w_mem_v):
    bp, sp, d = x_prompt.shape
    bs, ts, _ = x_sample.shape
    assert ts == 1
    depth = w_in.shape[0]
    n_mem = mem_prompt.shape[1]
    past_len = page_table.shape[1] * cache_ckv.shape[2]
    mem_k4 = jnp.transpose(cache_mem_k, (0, 1, 3, 4, 2)).reshape(depth, bs, GROUP_W, n_mem)
    mem_v4 = jnp.transpose(cache_mem_v, (0, 1, 3, 4, 2)).reshape(depth, bs, GROUP_W, n_mem)
    cache_krope_t = jnp.swapaxes(cache_krope, 2, 3)
    xp = x_prompt.reshape(bp * sp, d)
    xs = x_sample.reshape(bs, d)
    mem2 = mem_prompt.reshape(bp * n_mem, d)
    outs = [[] for _ in range(12)]
    for l in range(depth):
        lw = _prep_layer(l, pre_norm_g, post_norm_g, w_in, w_out, conv_w, rwkv_mu, rwkv_w0, rwkv_w2, rwkv_a0,
                         rwkv_a2, rwkv_k_k, rwkv_k_a, rwkv_r_k, rwkv_lnx_g, rwkv_lnx_b, mla_q_norm_g, mla_w_uq,
                         mla_kv_norm_g, mla_w_uk, mla_w_uv, mem_norm_g, w_mem_k, w_mem_v)
        mk, mv, mk_b, mv_b = _memkv_call(mem2, lw['mem_g'], lw['w_mem_kv'])
        (m_conv, conv_p, r_sh, r_gate, ckv, kr, kcat, qcat, mla_gate, m_mem) = _proj_call(
            xp, lw, decode=False, seq_len=sp,
            mk=mk_b.reshape(bp, n_mem, GROUP_W), mv=mv_b.reshape(bp, n_mem, GROUP_W))
        m_rwkv, st_p = _rwkv_prompt_call(r_sh, r_gate, lw, bp, sp)
        xp = _flash_call(qcat, kcat, mla_gate, lw['w_uv_pad'], xp, m_conv, m_rwkv, m_mem, lw['w_out'],
                         lw['post_g'], sp)
        sh_p = r_sh.reshape(bp, sp, RWKV_SHIFT_W)[:, -1]
        (s_conv, cn0, cn1, r_sh_s, r_gate_s, ckv_s, kr_s, _, qcat_s, mla_gate_s, mem_q_s, mem_gate_s) = _proj_call(
            xs, lw, decode=True, seq_len=1, pos0=float(past_len), conv_state=state_conv[l])
        s_rwkv, st_s = _rwkv_step_call(r_sh_s, state_rwkv_shift[l], r_gate_s, state_rwkv[l], lw)
        q16 = jnp.pad(jnp.transpose(qcat_s[0], (1, 0, 2)), ((0, 0), (0, 16 - N_HEADS), (0, 0)))
        s_mla = _decode_call(l, page_table, q16, ckv_s, kr_s, mla_gate_s, lw['w_uv_pad'], cache_ckv, cache_krope_t)
        s_mem = _memattn_call(l, mem_q_s, mem_gate_s, mem_k4, mem_v4)
        xs = _outproj_call(xs, (s_conv, s_rwkv, s_mla, s_mem), lw['w_out'], lw['post_g'])
        vals = (ckv.reshape(bp, sp, KV_RANK), ckv_s.reshape(bs, ts, KV_RANK),
                kr.reshape(bp, sp, ROPE_DIM), kr_s.reshape(bs, ts, ROPE_DIM),
                mk.reshape(bp, n_mem, N_HEADS, HEAD_DIM), mv.reshape(bp, n_mem, N_HEADS, HEAD_DIM),
                conv_p, jnp.stack([cn0, cn1], axis=1), sh_p, r_sh_s, st_p, st_s)
        for o, v in zip(outs, vals):
            o.append(v)
    return (xp.reshape(bp, sp, d), xs.reshape(bs, ts, d)) + tuple(jnp.stack(o) for o in outs)
```

```python
import functools

import jax
import jax.numpy as jnp
import numpy as np
from jax import lax
from jax.experimental import pallas as pl
from jax.experimental.pallas import tpu as pltpu

F32 = jnp.float32
BF16 = jnp.bfloat16

HEAD_DIM = 64
N_HEADS = 4
GROUP_W = N_HEADS * HEAD_DIM
CONV_K = 3
LORA_W = 64
RWKV_SHIFT_W = 3 * GROUP_W + 2 * LORA_W
Q_RANK = 256
KV_RANK = 128
NOPE_DIM = 64
ROPE_DIM = 32
QK_DIM = KV_RANK + ROPE_DIM
QK_W = 2 * KV_RANK
ONES_LANE = QK_DIM
RMS_EPS = 1e-6
LNX_EPS = 64e-5
ROPE_BASE = 10000.0
DECAY_SCALE = 0.6065306597
MLA_SCALE = (NOPE_DIM + ROPE_DIM) ** -0.5
MEM_SCALE = HEAD_DIM ** -0.5
NEG_INF = -1e30

LANES = 128
CHUNK = 64
SUB = 16
VMEM_LIMIT = 56 * 1024 * 1024
PROJ_ROWS = 512
RWKV_ROWS = 512
FLASH_ROWS = 512
DECODE_PAGES = 128
STEP_TOKENS = 8
MEMATTN_TOKENS = 4

_C_CONV, _C_RSH, _C_RGATE, _C_QDOWN, _C_CKV, _C_MLAG, _C_MEMQ, _C_MEMG, _C_KR, _C_END = (
    0, 1024, 1920, 2176, 2432, 2560, 2816, 3072, 3328, 3456)


def _mm(a, b):
    return jnp.dot(a.astype(BF16), b.astype(BF16), preferred_element_type=F32)


def _mm_nt(a, b):
    return lax.dot_general(a.astype(BF16), b.astype(BF16), (((1,), (1,)), ((), ())),
                           preferred_element_type=F32)


def _mm_tn(a, b):
    return lax.dot_general(a.astype(BF16), b.astype(BF16), (((0,), (0,)), ((), ())),
                           preferred_element_type=F32)


def _split_bf16(x, n):
    parts = []
    for _ in range(n):
        p = x.astype(BF16)
        parts.append(p)
        x = x - p.astype(F32)
    return parts


def _mm_hi(a, b):
    (a1, a2), (b1, b2) = _split_bf16(a, 2), _split_bf16(b, 2)
    dot = lambda x, y: jnp.dot(x, y, preferred_element_type=F32)
    return dot(a1, b1) + (dot(a1, b2) + dot(a2, b1))


def _mm_exact_lhs(a01, b):
    a = a01.astype(BF16)
    b1, b2, b3 = _split_bf16(b, 3)
    dot = lambda x: jnp.dot(a, x, preferred_element_type=F32)
    return dot(b1) + (dot(b2) + dot(b3))


def _sigmoid(x):
    return 1.0 / (1.0 + jnp.exp(-x))


def _silu(x):
    return x * _sigmoid(x)


def _rms(x, g):
    return x * lax.rsqrt(jnp.mean(x * x, axis=-1, keepdims=True) + RMS_EPS) * g


def _params(sem):
    return pltpu.CompilerParams(dimension_semantics=sem, vmem_limit_bytes=VMEM_LIMIT)


def _const_spec(shape):
    nd = len(shape)
    return pl.BlockSpec(shape, lambda *_: (0,) * nd)


def _tile(n, pref):
    t = min(n, pref)
    assert n % t == 0, (n, pref)
    return t


def _proj_body(decode, tiles_per_seq, pos0, *refs):
    if decode:
        (x_ref, g_ref, w_ref, cw_ref, qg_ref, wuq_ref, kvg_ref, wuk_ref, inv_ref, cb0_ref, cb1_ref,
         mconv_ref, cn0_ref, cn1_ref, rsh_ref, rgate_ref, ckv_ref, kr_ref, kcat_ref, qcat_ref,
         mlag_ref, memq_ref, memg_ref) = refs
    else:
        (x_ref, g_ref, w_ref, cw_ref, qg_ref, wuq_ref, kvg_ref, wuk_ref, inv_ref, mk_ref, mv_ref,
         mconv_ref, cst_ref, rsh_ref, rgate_ref, ckv_ref, kr_ref, kcat_ref, qcat_ref,
         mlag_ref, mmem_ref, carry_ref, cos_row_ref, sin_row_ref) = refs
    tm = x_ref.shape[0]
    t = pl.program_id(0) % tiles_per_seq
    h = _rms(x_ref[...], g_ref[...]).astype(BF16)

    def proj(lo, hi):
        return jnp.dot(h, w_ref[:, lo:hi], preferred_element_type=F32)

    cv = proj(_C_CONV, _C_RSH)
    c_b, c_c, c_x, c_g = (cv[:, i * GROUP_W:(i + 1) * GROUP_W] for i in range(4))
    u = c_c * c_x
    w0, w1, w2 = cw_ref[0:1, :], cw_ref[1:2, :], cw_ref[2:3, :]
    if decode:
        b0, b1 = cb0_ref[...], cb1_ref[...]
        y = b0 * w0 + b1 * w1 + u * w2
        cn0_ref[...] = b1
        cn1_ref[...] = u
    else:
        @pl.when(t == 0)
        def _():
            carry_ref[...] = jnp.zeros_like(carry_ref)
        p0, p1 = carry_ref[0:1, :], carry_ref[1:2, :]
        row = lax.broadcasted_iota(jnp.int32, u.shape, 0)
        u1 = jnp.where(row == 0, p1, pltpu.roll(u, 1, 0))
        u2 = jnp.where(row == 0, p0, jnp.where(row == 1, p1, pltpu.roll(u, 2, 0)))
        y = u2 * w0 + u1 * w1 + u * w2
        carry_ref[0:2, :] = u[tm - 2:tm, :]
        cst_ref[0] = u[tm - 2:tm, :]
    mconv_ref[...] = (c_b * y * _silu(c_g)).astype(mconv_ref.dtype)

    rsh_ref[...] = proj(_C_RSH, _C_RGATE)
    rgate_ref[...] = _silu(proj(_C_RGATE, _C_QDOWN))

    if decode:
        ang = jnp.full((tm, 1), pos0, F32) * inv_ref[...]
        cos, sin = jnp.cos(ang), jnp.sin(ang)
    else:
        @pl.when(pl.program_id(0) == 0)
        def _():
            ang_row = lax.broadcasted_iota(jnp.int32, (tm, 1), 0).astype(F32) * inv_ref[...]
            cos_row_ref[...] = jnp.cos(ang_row)
            sin_row_ref[...] = jnp.sin(ang_row)
        ang_tile = (t * tm).astype(F32) * inv_ref[...]
        cos_t, sin_t = jnp.cos(ang_tile), jnp.sin(ang_tile)
        cos_r, sin_r = cos_row_ref[...], sin_row_ref[...]
        cos = cos_t * cos_r - sin_t * sin_r
        sin = sin_t * cos_r + cos_t * sin_r
    first = (lax.broadcasted_iota(jnp.int32, (tm, LANES), 1) % ROPE_DIM) < (ROPE_DIM // 2)

    def rope(v):
        partner = jnp.where(first, -pltpu.roll(v, LANES - ROPE_DIM // 2, 1), pltpu.roll(v, ROPE_DIM // 2, 1))
        return v * cos + partner * sin

    qn = _rms(proj(_C_QDOWN, _C_CKV), qg_ref[...])
    q = _mm(qn, wuq_ref[...])
    q_rope = rope(q[:, GROUP_W:GROUP_W + LANES]) * MLA_SCALE
    lane = lax.broadcasted_iota(jnp.int32, (tm, LANES), 1)
    q_lat = _mm(q[:, 0:GROUP_W], wuk_ref[...]) * MLA_SCALE
    for hh in range(N_HEADS):
        qcat_ref[0, hh, :, 0:KV_RANK] = q_lat[:, hh * KV_RANK:(hh + 1) * KV_RANK].astype(qcat_ref.dtype)
        q_r = q_rope if hh == 0 else pltpu.roll(q_rope, LANES - hh * ROPE_DIM, 1)
        qcat_ref[0, hh, :, KV_RANK:QK_W] = jnp.where(lane < ROPE_DIM, q_r, 0.0).astype(qcat_ref.dtype)
    ckv = _rms(proj(_C_CKV, _C_MLAG), kvg_ref[...])
    krope = rope(proj(_C_KR, _C_END))
    ckv_ref[...] = ckv
    kr_ref[...] = krope[:, 0:ROPE_DIM]
    kcat_ref[:, 0:KV_RANK] = ckv.astype(kcat_ref.dtype)
    kcat_ref[:, KV_RANK:QK_W] = jnp.where(lane == ONES_LANE - KV_RANK, 1.0, krope).astype(kcat_ref.dtype)
    mlag_ref[...] = _silu(proj(_C_MLAG, _C_MEMQ))

    mq = proj(_C_MEMQ, _C_MEMG) * MEM_SCALE
    mg = _silu(proj(_C_MEMG, _C_KR))
    if decode:
        memq_ref[...] = mq
        memg_ref[...] = mg
    else:
        mk, mv = mk_ref[0], mv_ref[0]
        head = lax.broadcasted_iota(jnp.int32, (tm, GROUP_W), 1) // HEAD_DIM
        acc = jnp.zeros((tm, GROUP_W), F32)
        for hh in range(N_HEADS):
            s = _mm_nt(jnp.where(head == hh, mq, 0.0), mk)
            p = jnp.exp(s - jnp.max(s, axis=-1, keepdims=True))
            p = p / jnp.sum(p, axis=-1, keepdims=True)
            acc = acc + jnp.where(head == hh, _mm(p, mv), 0.0)
        mmem_ref[...] = (acc * mg).astype(mmem_ref.dtype)


def _proj_call(x, lw, *, decode, seq_len, pos0=0.0, conv_state=None, mk=None, mv=None):
    m, d = x.shape
    tm = _tile(seq_len if not decode else m, PROJ_ROWS)
    tps = 1 if decode else seq_len // tm
    nb = m // (tm * tps)
    grid = (m // tm,)
    row = lambda w: pl.BlockSpec((tm, w), lambda i: (i, 0))
    consts = [lw['pre_g'], lw['w_in'], lw['conv_w'], lw['q_norm_g'], lw['w_uq'], lw['kv_norm_g'],
              lw['w_ukT'], lw['inv_full']]
    in_specs = [row(d)] + [_const_spec(c.shape) for c in consts]
    qcat_shape = (nb, N_HEADS, tm * tps, QK_W)
    qcat_spec = pl.BlockSpec((1, N_HEADS, tm, QK_W), lambda i: (i // tps, 0, i % tps, 0))
    sds = jax.ShapeDtypeStruct
    common_out = [
        (sds((m, RWKV_SHIFT_W), F32), row(RWKV_SHIFT_W)),
        (sds((m, GROUP_W), F32), row(GROUP_W)),
        (sds((m, KV_RANK), F32), row(KV_RANK)),
        (sds((m, ROPE_DIM), F32), row(ROPE_DIM)),
        (sds((m, QK_W), BF16), row(QK_W)),
        (sds(qcat_shape, BF16), qcat_spec),
        (sds((m, GROUP_W), F32), row(GROUP_W)),
    ]
    if decode:
        ins = [x] + consts + [conv_state[:, 0], conv_state[:, 1]]
        in_specs += [row(GROUP_W), row(GROUP_W)]
        outs = ([(sds((m, GROUP_W), BF16), row(GROUP_W)),
                 (sds((m, GROUP_W), F32), row(GROUP_W)), (sds((m, GROUP_W), F32), row(GROUP_W))]
                + common_out
                + [(sds((m, GROUP_W), F32), row(GROUP_W)), (sds((m, GROUP_W), F32), row(GROUP_W))])
        scratch = []
    else:
        ins = [x] + consts + [mk, mv]
        n_mem = mk.shape[1]
        in_specs += [pl.BlockSpec((1, n_mem, GROUP_W), lambda i: (i // tps, 0, 0))] * 2
        outs = ([(sds((m, GROUP_W), BF16), row(GROUP_W)),
                 (sds((nb, CONV_K - 1, GROUP_W), F32),
                  pl.BlockSpec((1, CONV_K - 1, GROUP_W), lambda i: (i // tps, 0, 0)))]
                + common_out
                + [(sds((m, GROUP_W), BF16), row(GROUP_W))])
        scratch = [pltpu.VMEM((8, GROUP_W), F32), pltpu.VMEM((tm, LANES), F32), pltpu.VMEM((tm, LANES), F32)]
    return pl.pallas_call(
        functools.partial(_proj_body, decode, tps, pos0),
        grid=grid, in_specs=in_specs,
        out_specs=[o[1] for o in outs], out_shape=[o[0] for o in outs],
        scratch_shapes=scratch, compiler_params=_params(("arbitrary",)),
        name="proj_decode" if decode else "proj_prompt",
    )(*ins)


def _memkv_body(x_ref, g_ref, w_ref, k_ref, v_ref, kb_ref, vb_ref):
    kv = _mm(_rms(x_ref[...], g_ref[...]), w_ref[...])
    k, v = kv[:, :GROUP_W], kv[:, GROUP_W:]
    k_ref[...] = k
    v_ref[...] = v
    kb_ref[...] = k.astype(BF16)
    vb_ref[...] = v.astype(BF16)


def _memkv_call(mem, g, w_kv):
    m, d = mem.shape
    tm = _tile(m, PROJ_ROWS)
    row = lambda w: pl.BlockSpec((tm, w), lambda i: (i, 0))
    sds = jax.ShapeDtypeStruct
    return pl.pallas_call(
        _memkv_body, grid=(m // tm,),
        in_specs=[row(d), _const_spec(g.shape), _const_spec(w_kv.shape)],
        out_specs=[row(GROUP_W)] * 4,
        out_shape=[sds((m, GROUP_W), F32), sds((m, GROUP_W), F32), sds((m, GROUP_W), BF16), sds((m, GROUP_W), BF16)],
        compiler_params=_params(("arbitrary",)), name="mem_kv",
    )(mem, g, w_kv)


def _head_ones():
    r = lax.broadcasted_iota(jnp.int32, (GROUP_W, GROUP_W), 0) // HEAD_DIM
    c = lax.broadcasted_iota(jnp.int32, (GROUP_W, GROUP_W), 1) // HEAD_DIM
    return (r == c).astype(F32)


def _rwkv_tokens(mixed, prm, ones_bd):
    w0, a0, k_k, k_a, r_k = (prm[i:i + 1, :] for i in range(5))
    w2p, a2p = prm[8:8 + LANES, :], prm[8 + LANES:8 + 2 * LANES, :]
    r, k, v = mixed[:, 0:GROUP_W], mixed[:, GROUP_W:2 * GROUP_W], mixed[:, 2 * GROUP_W:3 * GROUP_W]
    wa = mixed[:, 3 * GROUP_W:RWKV_SHIFT_W]
    logw = -DECAY_SCALE * _sigmoid(w0 + _mm_hi(jnp.tanh(wa), w2p))
    a = _sigmoid(a0 + _mm_hi(wa, a2p))
    kkr = k * k_k
    kk = kkr * lax.rsqrt(_mm(kkr * kkr, ones_bd) + 1e-12)
    k2 = k * (1.0 + (a - 1.0) * k_a)
    bonus = _mm(r * k2 * r_k, ones_bd) * v
    return r, k2, v, logw, kk, a, bonus


def _rwkv_finish(y, bonus, gate, prm, ones_bd):
    lnx_g, lnx_b = prm[5:6, :], prm[6:7, :]
    mean = _mm(y, ones_bd) * (1.0 / HEAD_DIM)
    yc = y - mean
    var = _mm(yc * yc, ones_bd) * (1.0 / HEAD_DIM)
    return ((yc * lax.rsqrt(var + LNX_EPS)) * lnx_g + lnx_b + bonus) * gate


def _stack_heads(x, head_lane):
    return jnp.concatenate([jnp.where(head_lane == hh, x, 0.0) for hh in range(N_HEADS)], axis=0)


def _unstack_heads(x):
    c = x.shape[0] // N_HEADS
    return x[0:c] + x[c:2 * c] + x[2 * c:3 * c] + x[3 * c:4 * c]


def _rwkv_prompt_body(x_ref, gate_ref, mu_ref, prm_ref, o_ref, st_ref, prev_ref, s_ref):
    ct = x_ref.shape[1]
    si = pl.program_id(1)

    @pl.when(si == 0)
    def _():
        prev_ref[...] = jnp.zeros_like(prev_ref)
        s_ref[...] = jnp.zeros_like(s_ref)

    x = x_ref[0]
    row = lax.broadcasted_iota(jnp.int32, x.shape, 0)
    prev = jnp.where(row == 0, prev_ref[0:1, :], pltpu.roll(x, 1, 0))
    prev_ref[0:1, :] = x[ct - 1:ct, :]
    mixed = x + (prev - x) * mu_ref[...]
    prm = prm_ref[...]
    ones_bd = _head_ones()
    r, k2, v, logw, kk, a, bonus = _rwkv_tokens(mixed, prm, ones_bd)
    b = kk * a

    n = N_HEADS * CHUNK
    head_lane = lax.broadcasted_iota(jnp.int32, (CHUNK, GROUP_W), 1) // HEAD_DIM
    rr = lax.broadcasted_iota(jnp.int32, (n, n), 0)
    cc = lax.broadcasted_iota(jnp.int32, (n, n), 1)
    same_sub = (rr // SUB) == (cc // SUB)
    same_head = (rr // CHUNK) == (cc // CHUNK)
    eye = (rr == cc).astype(F32)
    t_w = lax.broadcasted_iota(jnp.int32, (CHUNK, GROUP_W), 0)
    s_w = lax.broadcasted_iota(jnp.int32, (CHUNK, GROUP_W), 1) % CHUNK
    strict_w, incl_w = t_w > s_w, t_w >= s_w
    eye_w = (t_w == s_w).astype(F32)
    eye_sub = (lax.broadcasted_iota(jnp.int32, (SUB, GROUP_W), 0)
               == lax.broadcasted_iota(jnp.int32, (SUB, GROUP_W), 1) % SUB).astype(F32)
    tri = (lax.broadcasted_iota(jnp.int32, (CHUNK, CHUNK), 0)
           >= lax.broadcasted_iota(jnp.int32, (CHUNK, CHUNK), 1)).astype(F32)

    def sub_square(z):
        return jnp.where(same_sub, jnp.concatenate([z] * (n // SUB), axis=0), 0.0)

    def sub_strip(z):
        out = z[0:SUB]
        for i in range(1, n // SUB):
            out = out + z[i * SUB:(i + 1) * SUB]
        return out

    chunks = range(ct // CHUNK)
    each = lambda f, *xs: [f(*(x[c] for x in xs)) for c in chunks]
    rows = lambda z: [z[c * CHUNK:(c + 1) * CHUNK] for c in chunks]
    stack = lambda z: _stack_heads(z, head_lane)
    lw = rows(logw)
    cw = each(lambda l: _mm_exact_lhs(tri, l), lw)
    cwl = each(lambda z: z[CHUNK - 1:CHUNK, :], cw)
    e_ng = each(lambda z: jnp.exp(-z), cw)
    e_w = each(lambda z, zl: jnp.exp(zl - z), cw, cwl)
    kkd = each(lambda x, z, l: x * jnp.exp(z - l), rows(kk), cw, lw)
    rd = each(lambda x, z: x * jnp.exp(z), rows(r), cw)
    kinvw = each(lambda x, e: x * e, rows(k2), e_w)
    binvw = each(lambda x, e: x * e, rows(b), e_w)
    v_w = rows(v)
    kkd_sq = each(stack, kkd)
    kinv_sq = each(lambda x, e: stack(x * e), rows(k2), e_ng)
    binv_sq = each(lambda x, e: stack(x * e), rows(b), e_ng)
    v_sq = each(stack, v_w)
    kr_rows = each(lambda x, y: jnp.concatenate([x, y], axis=0), kkd, rd)
    ab = each(_mm_nt, kr_rows, binv_sq)
    ak = each(_mm_nt, kr_rows, kinv_sq)
    a_bk = each(lambda z: jnp.where(strict_w, z[0:CHUNK], 0.0), ab)
    m_rb = each(lambda z: jnp.where(incl_w, z[CHUNK:2 * CHUNK], 0.0), ab)
    a_vk = each(lambda z: jnp.where(strict_w, z[0:CHUNK], 0.0), ak)
    m_rk = each(lambda z: jnp.where(incl_w, z[CHUNK:2 * CHUNK], 0.0), ak)
    a_sq = each(stack, a_bk)
    d1_sq = each(lambda x: jnp.where(same_sub, x, 0.0), a_sq)
    e1_sq = each(lambda x, y: x - y, a_sq, d1_sq)
    d1 = each(sub_strip, d1_sq)
    d2 = each(_mm, d1, d1_sq)
    t0 = each(lambda x: eye_sub - x, d1)
    r2 = each(lambda x, y, z: _mm(jnp.concatenate([x, y], axis=0), sub_square(z)), d2, t0, d2)
    d4 = each(lambda z: z[0:SUB], r2)
    t1 = each(lambda x, z: x + z[SUB:2 * SUB], t0, r2)
    r4 = each(lambda x, y, z: _mm(jnp.concatenate([x, y], axis=0), sub_square(z)), d4, t1, d4)
    d8 = each(lambda z: z[0:SUB], r4)
    t2 = each(lambda x, z: x + z[SUB:2 * SUB], t1, r4)
    dinv = each(lambda x, y: x + _mm(x, sub_square(y)), t2, d8)
    dinv_sq = each(sub_square, dinv)
    dinv_w = each(_unstack_heads, dinv_sq)
    nn = each(_mm, dinv_w, e1_sq)
    nn_sq = each(stack, nn)
    n2 = each(_mm, nn, nn_sq)
    t3 = each(lambda x, y: _mm(eye_w - x, eye + stack(y)), nn, n2)
    tinv = each(_mm, t3, dinv_sq)
    xv = each(lambda x, y, z: _mm(jnp.concatenate([x, y], axis=0), z), a_vk, m_rk, v_sq)
    x_w = each(lambda z: z[0:CHUNK], xv)
    kkdp = each(_mm, tinv, kkd_sq)
    vp = each(_mm, tinv, each(stack, x_w))
    kkdp_sq = each(stack, kkdp)
    rq = each(lambda x, y, z: x - _mm(y, z), rd, m_rb, kkdp_sq)
    y_in = each(lambda z, a3, a4: z[CHUNK:2 * CHUNK] - _mm(a3, stack(a4)), xv, m_rb, vp)
    phi = each(lambda zl, x, y: eye * jnp.exp(zl) - jnp.where(same_head, _mm_tn(x, y), 0.0), cwl, kkdp, binvw)
    g = each(lambda a1, a2, a3, a4: _unstack_heads(jnp.where(
        same_head, _mm_tn(jnp.concatenate([a1, -a3], axis=0), jnp.concatenate([a2, a4], axis=0)), 0.0)),
        v_w, kinvw, vp, binvw)
    state = s_ref[...]
    ys = []
    for c in chunks:
        ys.append(_mm_nt(rq[c], stack(state)) + y_in[c])
        state = _mm(state, phi[c]) + g[c]
    s_ref[...] = state
    st_ref[0] = state
    y = jnp.concatenate(ys, axis=0) if len(ys) > 1 else ys[0]
    o_ref[...] = _rwkv_finish(y, bonus, gate_ref[...], prm, ones_bd).astype(o_ref.dtype)


def _rwkv_prompt_call(r_sh, gate, lw, nb, seq_len):
    ct = _tile(seq_len, RWKV_ROWS)
    ns = seq_len // ct
    x3 = r_sh.reshape(nb, seq_len, RWKV_SHIFT_W)
    sds = jax.ShapeDtypeStruct
    out, st = pl.pallas_call(
        _rwkv_prompt_body, grid=(nb, ns),
        in_specs=[pl.BlockSpec((1, ct, RWKV_SHIFT_W), lambda b, s: (b, s, 0)),
                  pl.BlockSpec((ct, GROUP_W), lambda b, s: (b * ns + s, 0)),
                  _const_spec(lw['mu'].shape), _const_spec(lw['rwkv_prm'].shape)],
        out_specs=[pl.BlockSpec((ct, GROUP_W), lambda b, s: (b * ns + s, 0)),
                   pl.BlockSpec((1, HEAD_DIM, GROUP_W), lambda b, s: (b, 0, 0))],
        out_shape=[sds((nb * seq_len, GROUP_W), BF16), sds((nb, HEAD_DIM, GROUP_W), F32)],
        scratch_shapes=[pltpu.VMEM((8, RWKV_SHIFT_W), F32),
                        pltpu.VMEM((HEAD_DIM, GROUP_W), F32)],
        compiler_params=_params(("arbitrary", "arbitrary")), name="rwkv_prompt",
    )(x3, gate, lw['mu'], lw['rwkv_prm'])
    st = st.reshape(nb, HEAD_DIM, N_HEADS, HEAD_DIM).transpose(0, 2, 1, 3)
    return out, st


def _rwkv_step_body(x_ref, prev_ref, gate_ref, mu_ref, prm_ref, s_ref, o_ref, so_ref):
    tb = x_ref.shape[0]
    x = x_ref[...]
    mixed = x + (prev_ref[...] - x) * mu_ref[...]
    prm = prm_ref[...]
    ones_bd = _head_ones()
    r, k2, v, logw, kk, a, bonus = _rwkv_tokens(mixed, prm, ones_bd)
    eye_w = (lax.broadcasted_iota(jnp.int32, (HEAD_DIM, GROUP_W), 0)
             == lax.broadcasted_iota(jnp.int32, (HEAD_DIM, GROUP_W), 1) % HEAD_DIM).astype(F32)

    def rows(z):
        return jnp.concatenate([jnp.broadcast_to(z[bi:bi + 1, :], (HEAD_DIM, GROUP_W)) for bi in range(tb)], axis=0)

    eye_rows = jnp.concatenate([eye_w] * tb, axis=0)
    st = s_ref[...]
    sa = _mm(st * rows(kk), ones_bd)
    v_col = _mm_hi(eye_rows * rows(v), ones_bd)
    st = st * rows(jnp.exp(logw)) - sa * rows(kk * a) + v_col * rows(k2)
    so_ref[...] = st
    y_rep = _mm(st * rows(r), ones_bd) * eye_rows
    y = jnp.concatenate([jnp.sum(y_rep[bi * HEAD_DIM:(bi + 1) * HEAD_DIM], axis=0, keepdims=True)
                         for bi in range(tb)], axis=0)
    o_ref[...] = _rwkv_finish(y, bonus, gate_ref[...], prm, ones_bd).astype(o_ref.dtype)


def _rwkv_step_call(r_sh, prev, gate, state, lw):
    m = r_sh.shape[0]
    tb = _tile(m, STEP_TOKENS)
    row = lambda w: pl.BlockSpec((tb, w), lambda i: (i, 0))
    st_spec = pl.BlockSpec((tb * HEAD_DIM, GROUP_W), lambda i: (i, 0))
    state_w = jnp.transpose(state, (0, 2, 1, 3)).reshape(m * HEAD_DIM, GROUP_W)
    sds = jax.ShapeDtypeStruct
    out, st = pl.pallas_call(
        _rwkv_step_body, grid=(m // tb,),
        in_specs=[row(RWKV_SHIFT_W), row(RWKV_SHIFT_W), row(GROUP_W),
                  _const_spec(lw['mu'].shape), _const_spec(lw['rwkv_prm'].shape), st_spec],
        out_specs=[row(GROUP_W), st_spec],
        out_shape=[sds((m, GROUP_W), BF16), sds(state_w.shape, F32)],
        compiler_params=_params(("arbitrary",)), name="rwkv_step",
    )(r_sh, prev, gate, lw['mu'], lw['rwkv_prm'], state_w)
    return out, jnp.transpose(st.reshape(m, HEAD_DIM, N_HEADS, HEAD_DIM), (0, 2, 1, 3))


def _softmax_update(s, v, m_sc, l_sc, acc_sc):
    m_prev = m_sc[...]
    m_new = jnp.maximum(m_prev, jnp.max(s, axis=-1, keepdims=True))
    alpha = jnp.exp(m_prev - m_new)
    p = jnp.exp(s - m_new)
    l_sc[...] = alpha * l_sc[...] + jnp.sum(p, axis=-1, keepdims=True)
    acc_sc[...] = alpha * acc_sc[...] + _mm(p, v)
    m_sc[...] = m_new


def _flash_body(q_ref, k_ref, gate_ref, wuv_ref, x_ref, mconv_ref, mrwkv_ref, mmem_ref, wout_ref, pg_ref,
                o_ref, sa_ref, sb_ref, m_ref, acc_ref):
    qi = pl.program_id(1)
    tq = q_ref.shape[2]

    def keys(kb):
        return k_ref[0, pl.ds(pl.multiple_of(kb * tq, tq), tq), :]

    def scores(kb, s_ref):
        k = keys(kb)
        for hh in range(N_HEADS):
            s_ref[hh] = _mm_nt(q_ref[0, hh], k)

    def update(s_ref, kb, masked):
        k = keys(kb)
        for hh in range(N_HEADS):
            s_h = s_ref[hh]
            if masked:
                s_h = jnp.where(lax.broadcasted_iota(jnp.int32, s_h.shape, 1)
                                <= lax.broadcasted_iota(jnp.int32, s_h.shape, 0), s_h, NEG_INF)
            m_prev = m_ref[hh]
            m_new = jnp.maximum(m_prev, jnp.max(s_h, axis=-1, keepdims=True))
            p = jnp.exp(s_h - m_new)
            acc_ref[hh] = jnp.exp(m_prev - m_new) * acc_ref[hh] + _mm(p, k)
            m_ref[hh] = m_new

    def finish(s_ref):
        update(s_ref, qi, True)
        mla = jnp.zeros((tq, GROUP_W), F32)
        for hh in range(N_HEADS):
            acc = acc_ref[hh]
            o = acc[:, 0:KV_RANK] / acc[:, ONES_LANE:ONES_LANE + 1]
            mla = mla + _mm(o, wuv_ref[hh])
        mixed = (mconv_ref[...], mrwkv_ref[...], (mla * gate_ref[...]).astype(BF16), mmem_ref[...])
        y = jnp.zeros(x_ref.shape, F32)
        for i, m_i in enumerate(mixed):
            y = y + _mm(m_i, wout_ref[i * GROUP_W:(i + 1) * GROUP_W, :])
        o_ref[...] = x_ref[...] + _rms(y, pg_ref[...])

    m_ref[...] = jnp.full_like(m_ref, NEG_INF)
    acc_ref[...] = jnp.zeros_like(acc_ref)
    scores(0, sa_ref)

    def pair(j, _):
        kb = 2 * j
        scores(kb + 1, sb_ref)
        update(sa_ref, kb, False)
        scores(kb + 2, sa_ref)
        update(sb_ref, kb + 1, False)
        return 0

    lax.fori_loop(0, qi // 2, pair, 0)

    @pl.when(qi % 2 == 1)
    def _():
        scores(qi, sb_ref)
        update(sa_ref, qi - 1, False)
        finish(sb_ref)

    @pl.when(qi % 2 == 0)
    def _():
        finish(sa_ref)


def _flash_call(qcat, kcat, gate, w_uv_pad, x, m_conv, m_rwkv, m_mem, w_out, post_g, seq_len):
    nb = qcat.shape[0]
    d = x.shape[1]
    tq = _tile(seq_len, FLASH_ROWS)
    nq = seq_len // tq
    k3 = kcat.reshape(nb, seq_len, QK_W)
    row = lambda w: pl.BlockSpec((tq, w), lambda b, i: (b * nq + i, 0))
    return pl.pallas_call(
        _flash_body, grid=(nb, nq),
        in_specs=[pl.BlockSpec((1, N_HEADS, tq, QK_W), lambda b, i: (b, 0, i, 0)),
                  pl.BlockSpec((1, seq_len, QK_W), lambda b, i: (b, 0, 0)),
                  row(GROUP_W), _const_spec(w_uv_pad.shape),
                  row(d), row(GROUP_W), row(GROUP_W), row(GROUP_W),
                  _const_spec(w_out.shape), _const_spec(post_g.shape)],
        out_specs=row(d),
        out_shape=jax.ShapeDtypeStruct((nb * seq_len, d), F32),
        scratch_shapes=[pltpu.VMEM((N_HEADS, tq, tq), F32), pltpu.VMEM((N_HEADS, tq, tq), F32),
                        pltpu.VMEM((N_HEADS, tq, 1), F32), pltpu.VMEM((N_HEADS, tq, QK_W), F32)],
        compiler_params=_params(("arbitrary", "arbitrary")), name="flash_mla",
    )(qcat, k3, gate, w_uv_pad, x, m_conv, m_rwkv, m_mem, w_out, post_g)


def _decode_body(layer, nch, pg, pt_ref, q_ref, cnew_ref, knew_ref, gate_ref, wuv_ref, ckv_hbm, kr_hbm,
                 o_ref, cbuf, kbuf, sem, m_sc, l_sc, acc_sc):
    b, c = pl.program_id(0), pl.program_id(1)
    step = b * nch + c
    total = pl.num_programs(0) * nch
    slot = step % 2
    page_len = cbuf.shape[2]

    def page_copies(st, sl):
        bb, cc = st // nch, st % nch
        out = []
        for p in range(pg):
            page = pt_ref[bb, cc * pg + p]
            out.append(pltpu.make_async_copy(ckv_hbm.at[layer, page], cbuf.at[sl, p], sem.at[0, sl]))
            out.append(pltpu.make_async_copy(kr_hbm.at[layer, page], kbuf.at[sl, :, pl.ds(p * page_len, page_len)],
                                             sem.at[1, sl]))
        return out

    @pl.when(step == 0)
    def _():
        for cp in page_copies(0, 0):
            cp.start()

    @pl.when(step + 1 < total)
    def _():
        for cp in page_copies(step + 1, 1 - slot):
            cp.start()

    @pl.when(c == 0)
    def _():
        m_sc[...] = jnp.full_like(m_sc, NEG_INF)
        l_sc[...] = jnp.zeros_like(l_sc)
        acc_sc[...] = jnp.zeros_like(acc_sc)

    for cp in page_copies(step, slot):
        cp.wait()

    ck = cbuf[slot].reshape(pg * page_len, KV_RANK).astype(BF16)
    q = q_ref[0]
    s = _mm_nt(q[:, 0:KV_RANK], ck) + _mm(q[:, KV_RANK:QK_DIM], kbuf[slot])
    _softmax_update(s, ck, m_sc, l_sc, acc_sc)

    @pl.when(c == nch - 1)
    def _():
        qf = q.astype(F32)
        cn, kn = cnew_ref[0], knew_ref[0]
        s_new = (jnp.sum(qf[:, 0:KV_RANK] * cn, axis=-1, keepdims=True)
                 + jnp.sum(qf[:, KV_RANK:QK_DIM] * kn, axis=-1, keepdims=True))
        m_prev = m_sc[...]
        m_new = jnp.maximum(m_prev, s_new)
        alpha, p_new = jnp.exp(m_prev - m_new), jnp.exp(s_new - m_new)
        l_fin = alpha * l_sc[...] + p_new
        o = (alpha * acc_sc[...] + p_new * cn) / l_fin
        out = jnp.zeros((1, GROUP_W), F32)
        for hh in range(N_HEADS):
            out = out + _mm(o, wuv_ref[hh])[hh:hh + 1, :]
        o_ref[0] = out * gate_ref[0]


def _decode_call(layer, page_table, q16, ckv_new, kr_new, gate, w_uv_pad, cache_ckv, cache_krope):
    nb, n_pages = page_table.shape
    page = cache_ckv.shape[2]
    pg = _tile(n_pages, DECODE_PAGES)
    nch = n_pages // pg
    blk = lambda shape: pl.BlockSpec((1,) + shape, lambda b, c, pt: (b, 0, 0))
    grid_spec = pltpu.PrefetchScalarGridSpec(
        num_scalar_prefetch=1, grid=(nb, nch),
        in_specs=[blk((16, QK_W)), blk((1, KV_RANK)), blk((1, ROPE_DIM)), blk((1, GROUP_W)),
                  pl.BlockSpec(w_uv_pad.shape, lambda b, c, pt: (0, 0, 0)),
                  pl.BlockSpec(memory_space=pl.ANY), pl.BlockSpec(memory_space=pl.ANY)],
        out_specs=blk((1, GROUP_W)),
        scratch_shapes=[pltpu.VMEM((2, pg, page, KV_RANK), F32), pltpu.VMEM((2, ROPE_DIM, pg * page), F32),
                        pltpu.SemaphoreType.DMA((2, 2)),
                        pltpu.VMEM((16, 1), F32), pltpu.VMEM((16, 1), F32), pltpu.VMEM((16, KV_RANK), F32)])
    out = pl.pallas_call(
        functools.partial(_decode_body, layer, nch, pg), grid_spec=grid_spec,
        out_shape=jax.ShapeDtypeStruct((nb, 1, GROUP_W), F32),
        compiler_params=_params(("arbitrary", "arbitrary")), name="paged_decode",
    )(page_table, q16, ckv_new[:, None, :], kr_new[:, None, :], gate[:, None, :], w_uv_pad,
      cache_ckv, cache_krope)
    return out.reshape(nb, GROUP_W)


def _memattn_body(q_ref, gate_ref, k_ref, v_ref, o_ref):
    tb = q_ref.shape[0]
    head = lax.broadcasted_iota(jnp.int32, (8, GROUP_W), 1) // HEAD_DIM
    sel = head == lax.broadcasted_iota(jnp.int32, (8, GROUP_W), 0)
    for bi in range(tb):
        q_bd = jnp.where(sel, q_ref[bi], 0.0)
        s = _mm(q_bd, k_ref[0, bi])
        p = jnp.exp(s - jnp.max(s, axis=-1, keepdims=True))
        p = p / jnp.sum(p, axis=-1, keepdims=True)
        o = jnp.where(sel, _mm_nt(p, v_ref[0, bi]), 0.0)
        o_ref[bi] = jnp.sum(o, axis=0, keepdims=True) * gate_ref[bi]


def _memattn_call(layer, mem_q, gate, mem_k, mem_v):
    m = mem_q.shape[0]
    n_mem = mem_k.shape[3]
    tb = _tile(m, MEMATTN_TOKENS)
    vec = pl.BlockSpec((tb, 1, GROUP_W), lambda i: (i, 0, 0))
    kv = pl.BlockSpec((1, tb, GROUP_W, n_mem), lambda i: (layer, i, 0, 0))
    out = pl.pallas_call(
        _memattn_body, grid=(m // tb,),
        in_specs=[vec, vec, kv, kv], out_specs=vec,
        out_shape=jax.ShapeDtypeStruct((m, 1, GROUP_W), F32),
        compiler_params=_params(("arbitrary",)), name="mem_attn_decode",
    )(mem_q[:, None, :], gate[:, None, :], mem_k, mem_v)
    return out.reshape(m, GROUP_W)


def _outproj_body(x_ref, m0_ref, m1_ref, m2_ref, m3_ref, w_ref, g_ref, o_ref):
    acc = jnp.zeros(x_ref.shape, F32)
    for i, m_ref in enumerate((m0_ref, m1_ref, m2_ref, m3_ref)):
        acc = acc + _mm(m_ref[...], w_ref[i * GROUP_W:(i + 1) * GROUP_W, :])
    o_ref[...] = x_ref[...] + _rms(acc, g_ref[...])


def _outproj_call(x, mixed, w_out, g):
    m, d = x.shape
    tm = _tile(m, PROJ_ROWS)
    row = lambda w: pl.BlockSpec((tm, w), lambda i: (i, 0))
    return pl.pallas_call(
        _outproj_body, grid=(m // tm,),
        in_specs=[row(d)] + [row(GROUP_W)] * 4 + [_const_spec(w_out.shape), _const_spec(g.shape)],
        out_specs=row(d), out_shape=jax.ShapeDtypeStruct((m, d), F32),
        compiler_params=_params(("arbitrary",)), name="out_proj",
    )(x, *mixed, w_out, g)


def _prep_layer(l, pre_norm_g, post_norm_g, w_in, w_out, conv_w, rwkv_mu, rwkv_w0, rwkv_w2, rwkv_a0, rwkv_a2,
                rwkv_k_k, rwkv_k_a, rwkv_r_k, rwkv_lnx_g, rwkv_lnx_b, mla_q_norm_g, mla_w_uq, mla_kv_norm_g,
                mla_w_uk, mla_w_uv, mem_norm_g, w_mem_k, w_mem_v):
    d = w_in.shape[1]
    kr0 = _C_MLAG
    w = w_in[l]
    w_in_p = jnp.concatenate([w[:, :kr0], w[:, kr0 + ROPE_DIM:], w[:, kr0:kr0 + ROPE_DIM],
                              jnp.zeros((d, LANES - ROPE_DIM), F32)], axis=1).astype(BF16)
    uq = mla_w_uq[l].reshape(Q_RANK, N_HEADS, NOPE_DIM + ROPE_DIM)
    w_uq = jnp.concatenate([uq[:, :, :NOPE_DIM].reshape(Q_RANK, -1), uq[:, :, NOPE_DIM:].reshape(Q_RANK, -1)],
                           axis=1).astype(BF16)
    w_uv_pad = jnp.zeros((N_HEADS, KV_RANK, GROUP_W), F32)
    for hh in range(N_HEADS):
        w_uv_pad = w_uv_pad.at[hh, :, hh * HEAD_DIM:(hh + 1) * HEAD_DIM].set(mla_w_uv[l][:, hh, :])
    w_uk_bd = jnp.zeros((N_HEADS * NOPE_DIM, N_HEADS * KV_RANK), F32)
    for hh in range(N_HEADS):
        w_uk_bd = w_uk_bd.at[hh * NOPE_DIM:(hh + 1) * NOPE_DIM, hh * KV_RANK:(hh + 1) * KV_RANK].set(
            mla_w_uk[l][:, hh, :].T)
    half = ROPE_DIM // 2
    inv = jnp.power(ROPE_BASE, -jnp.arange(half, dtype=F32) / half)
    zl = jnp.zeros((LORA_W, GROUP_W), F32)
    vecs = jnp.stack([rwkv_w0[l], rwkv_a0[l], rwkv_k_k[l], rwkv_k_a[l], rwkv_r_k[l].reshape(-1),
                      rwkv_lnx_g[l], rwkv_lnx_b[l], jnp.zeros((GROUP_W,), F32)])
    return {
        'pre_g': pre_norm_g[l][None], 'post_g': post_norm_g[l][None], 'w_in': w_in_p,
        'w_out': w_out[l].astype(BF16), 'conv_w': conv_w[l], 'q_norm_g': mla_q_norm_g[l][None], 'w_uq': w_uq,
        'kv_norm_g': mla_kv_norm_g[l][None], 'w_ukT': w_uk_bd.astype(BF16),
        'w_uv_pad': w_uv_pad.astype(BF16), 'inv_full': jnp.tile(inv, LANES // half)[None],
        'mu': rwkv_mu[l][None],
        'rwkv_prm': jnp.concatenate([vecs, rwkv_w2[l], zl, zl, rwkv_a2[l]], axis=0),
        'mem_g': mem_norm_g[l][None],
        'w_mem_kv': jnp.concatenate([w_mem_k[l], w_mem_v[l]], axis=1).astype(BF16),
    }


def kernel(x_prompt, x_sample, cache_ckv, cache_krope, cache_mem_k, cache_mem_v, state_conv, state_rwkv_shift, state_rwkv, page_table, mem_prompt, pre_norm_g, post_norm_g, w_in, w_out, conv_w, rwkv_mu, rwkv_w0, rwkv_w2, rwkv_a0, rwkv_a2, rwkv_k_k, rwkv_k_a, rwkv_r_k, rwkv_lnx_g, rwkv_lnx_b, mla_q_norm_g, mla_w_uq, mla_kv_norm_g, mla_w_uk, mla_w_uv, mem_norm_g, w_mem_k, w_mem_v):
    bp, sp, d = x_prompt.shape
    bs, ts, _ = x_sample.shape
    assert ts == 1
    depth = w_in.shape[0]
    n_mem = mem_prompt.shape[1]
    past_len = page_table.shape[1] * cache_ckv.shape[2]
    mem_k4 = jnp.transpose(cache_mem_k, (0, 1, 3, 4, 2)).reshape(depth, bs, GROUP_W, n_mem)
    mem_v4 = jnp.transpose(cache_mem_v, (0, 1, 3, 4, 2)).reshape(depth, bs, GROUP_W, n_mem)
    cache_krope_t = jnp.swapaxes(cache_krope, 2, 3)
    xp = x_prompt.reshape(bp * sp, d)
    xs = x_sample.reshape(bs, d)
    mem2 = mem_prompt.reshape(bp * n_mem, d)
    outs = [[] for _ in range(12)]
    for l in range(depth):
        lw = _prep_layer(l, pre_norm_g, post_norm_g, w_in, w_out, conv_w, rwkv_mu, rwkv_w0, rwkv_w2, rwkv_a0,
                         rwkv_a2, rwkv_k_k, rwkv_k_a, rwkv_r_k, rwkv_lnx_g, rwkv_lnx_b, mla_q_norm_g, mla_w_uq,
                         mla_kv_norm_g, mla_w_uk, mla_w_uv, mem_norm_g, w_mem_k, w_mem_v)
        mk, mv, mk_b, mv_b = _memkv_call(mem2, lw['mem_g'], lw['w_mem_kv'])
        (m_conv, conv_p, r_sh, r_gate, ckv, kr, kcat, qcat, mla_gate, m_mem) = _proj_call(
            xp, lw, decode=False, seq_len=sp,
            mk=mk_b.reshape(bp, n_mem, GROUP_W), mv=mv_b.reshape(bp, n_mem, GROUP_W))
        m_rwkv, st_p = _rwkv_prompt_call(r_sh, r_gate, lw, bp, sp)
        xp = _flash_call(qcat, kcat, mla_gate, lw['w_uv_pad'], xp, m_conv, m_rwkv, m_mem, lw['w_out'],
                         lw['post_g'], sp)
        sh_p = r_sh.reshape(bp, sp, RWKV_SHIFT_W)[:, -1]
        (s_conv, cn0, cn1, r_sh_s, r_gate_s, ckv_s, kr_s, _, qcat_s, mla_gate_s, mem_q_s, mem_gate_s) = _proj_call(
            xs, lw, decode=True, seq_len=1, pos0=float(past_len), conv_state=state_conv[l])
        s_rwkv, st_s = _rwkv_step_call(r_sh_s, state_rwkv_shift[l], r_gate_s, state_rwkv[l], lw)
        q16 = jnp.pad(jnp.transpose(qcat_s[0], (1, 0, 2)), ((0, 0), (0, 16 - N_HEADS), (0, 0)))
        s_mla = _decode_call(l, page_table, q16, ckv_s, kr_s, mla_gate_s, lw['w_uv_pad'], cache_ckv, cache_krope_t)
        s_mem = _memattn_call(l, mem_q_s, mem_gate_s, mem_k4, mem_v4)
        xs = _outproj_call(xs, (s_conv, s_rwkv, s_mla, s_mem), lw['w_out'], lw['post_g'])
        vals = (ckv.reshape(bp, sp, KV_RANK), ckv_s.reshape(bs, ts, KV_RANK),
                kr.reshape(bp, sp, ROPE_DIM), kr_s.reshape(bs, ts, ROPE_DIM),
                mk.reshape(bp, n_mem, N_HEADS, HEAD_DIM), mv.reshape(bp, n_mem, N_HEADS, HEAD_DIM),
                conv_p, jnp.stack([cn0, cn1], axis=1), sh_p, r_sh_s, st_p, st_s)
        for o, v in zip(outs, vals):
            o.append(v)
    return (xp.reshape(bp, sp, d), xs.reshape(bs, ts, d)) + tuple(jnp.stack(o) for o in outs)
```

```python
import functools

import jax
import jax.numpy as jnp
import numpy as np
from jax import lax
from jax.experimental import pallas as pl
from jax.experimental.pallas import tpu as pltpu

F32 = jnp.float32
BF16 = jnp.bfloat16

HEAD_DIM = 64
N_HEADS = 4
GROUP_W = N_HEADS * HEAD_DIM
CONV_K = 3
LORA_W = 64
RWKV_SHIFT_W = 3 * GROUP_W + 2 * LORA_W
Q_RANK = 256
KV_RANK = 128
NOPE_DIM = 64
ROPE_DIM = 32
QK_DIM = KV_RANK + ROPE_DIM
QK_W = 2 * KV_RANK
ONES_LANE = QK_DIM
RMS_EPS = 1e-6
LNX_EPS = 64e-5
ROPE_BASE = 10000.0
DECAY_SCALE = 0.6065306597
MLA_SCALE = (NOPE_DIM + ROPE_DIM) ** -0.5
MEM_SCALE = HEAD_DIM ** -0.5
NEG_INF = -1e30

LANES = 128
CHUNK = 64
SUB = 16
VMEM_LIMIT = 56 * 1024 * 1024
PROJ_ROWS = 512
RWKV_ROWS = 512
FLASH_ROWS = 512
DECODE_PAGES = 128
MEMATTN_TOKENS = 8

_C_CONV, _C_RSH, _C_RGATE, _C_QDOWN, _C_CKV, _C_MLAG, _C_MEMQ, _C_MEMG, _C_KR, _C_END = (
    0, 1024, 1920, 2176, 2432, 2560, 2816, 3072, 3328, 3456)


def _mm(a, b):
    return jnp.dot(a.astype(BF16), b.astype(BF16), preferred_element_type=F32)


def _mm_nt(a, b):
    return lax.dot_general(a.astype(BF16), b.astype(BF16), (((1,), (1,)), ((), ())),
                           preferred_element_type=F32)


def _mm_tn(a, b):
    return lax.dot_general(a.astype(BF16), b.astype(BF16), (((0,), (0,)), ((), ())),
                           preferred_element_type=F32)


def _split_bf16(x, n):
    parts = []
    for _ in range(n):
        p = x.astype(BF16)
        parts.append(p)
        x = x - p.astype(F32)
    return parts


def _mm_hi(a, b):
    (a1, a2), (b1, b2) = _split_bf16(a, 2), _split_bf16(b, 2)
    dot = lambda x, y: jnp.dot(x, y, preferred_element_type=F32)
    return dot(a1, b1) + (dot(a1, b2) + dot(a2, b1))


def _mm_exact_lhs(a01, b):
    a = a01.astype(BF16)
    b1, b2, b3 = _split_bf16(b, 3)
    dot = lambda x: jnp.dot(a, x, preferred_element_type=F32)
    return dot(b1) + (dot(b2) + dot(b3))


def _sigmoid(x):
    return 1.0 / (1.0 + jnp.exp(-x))


def _silu(x):
    return x * _sigmoid(x)


def _rms(x, g):
    return x * lax.rsqrt(jnp.mean(x * x, axis=-1, keepdims=True) + RMS_EPS) * g


def _params(sem):
    return pltpu.CompilerParams(dimension_semantics=sem, vmem_limit_bytes=VMEM_LIMIT)


def _const_spec(shape):
    nd = len(shape)
    return pl.BlockSpec(shape, lambda *_: (0,) * nd)


def _tile(n, pref):
    t = min(n, pref)
    assert n % t == 0, (n, pref)
    return t


def _proj_body(decode, tiles_per_seq, pos0, *refs):
    if decode:
        (x_ref, g_ref, w_ref, cw_ref, qg_ref, wuq_ref, kvg_ref, wuk_ref, inv_ref, cb0_ref, cb1_ref,
         mconv_ref, cn0_ref, cn1_ref, rsh_ref, rgate_ref, ckv_ref, kr_ref, kcat_ref, qcat_ref,
         mlag_ref, memq_ref, memg_ref) = refs
    else:
        (x_ref, g_ref, w_ref, cw_ref, qg_ref, wuq_ref, kvg_ref, wuk_ref, inv_ref, mk_ref, mv_ref,
         mconv_ref, cst_ref, rsh_ref, rgate_ref, ckv_ref, kr_ref, kcat_ref, qcat_ref,
         mlag_ref, mmem_ref, carry_ref, cos_row_ref, sin_row_ref) = refs
    tm = x_ref.shape[0]
    t = pl.program_id(0) % tiles_per_seq
    h = _rms(x_ref[...], g_ref[...]).astype(BF16)

    def proj(lo, hi):
        return jnp.dot(h, w_ref[:, lo:hi], preferred_element_type=F32)

    cv = proj(_C_CONV, _C_RSH)
    c_b, c_c, c_x, c_g = (cv[:, i * GROUP_W:(i + 1) * GROUP_W] for i in range(4))
    u = c_c * c_x
    w0, w1, w2 = cw_ref[0:1, :], cw_ref[1:2, :], cw_ref[2:3, :]
    if decode:
        b0, b1 = cb0_ref[...], cb1_ref[...]
        y = b0 * w0 + b1 * w1 + u * w2
        cn0_ref[...] = b1
        cn1_ref[...] = u
    else:
        @pl.when(t == 0)
        def _():
            carry_ref[...] = jnp.zeros_like(carry_ref)
        p0, p1 = carry_ref[0:1, :], carry_ref[1:2, :]
        row = lax.broadcasted_iota(jnp.int32, u.shape, 0)
        u1 = jnp.where(row == 0, p1, pltpu.roll(u, 1, 0))
        u2 = jnp.where(row == 0, p0, jnp.where(row == 1, p1, pltpu.roll(u, 2, 0)))
        y = u2 * w0 + u1 * w1 + u * w2
        carry_ref[0:2, :] = u[tm - 2:tm, :]
        cst_ref[0] = u[tm - 2:tm, :]
    mconv_ref[...] = (c_b * y * _silu(c_g)).astype(mconv_ref.dtype)

    rsh_ref[...] = proj(_C_RSH, _C_RGATE)
    rgate_ref[...] = _silu(proj(_C_RGATE, _C_QDOWN))

    if decode:
        ang = jnp.full((tm, 1), pos0, F32) * inv_ref[...]
        cos, sin = jnp.cos(ang), jnp.sin(ang)
    else:
        @pl.when(pl.program_id(0) == 0)
        def _():
            ang_row = lax.broadcasted_iota(jnp.int32, (tm, 1), 0).astype(F32) * inv_ref[...]
            cos_row_ref[...] = jnp.cos(ang_row)
            sin_row_ref[...] = jnp.sin(ang_row)
        ang_tile = (t * tm).astype(F32) * inv_ref[...]
        cos_t, sin_t = jnp.cos(ang_tile), jnp.sin(ang_tile)
        cos_r, sin_r = cos_row_ref[...], sin_row_ref[...]
        cos = cos_t * cos_r - sin_t * sin_r
        sin = sin_t * cos_r + cos_t * sin_r
    first = (lax.broadcasted_iota(jnp.int32, (tm, LANES), 1) % ROPE_DIM) < (ROPE_DIM // 2)

    def rope(v):
        partner = jnp.where(first, -pltpu.roll(v, LANES - ROPE_DIM // 2, 1), pltpu.roll(v, ROPE_DIM // 2, 1))
        return v * cos + partner * sin

    qn = _rms(proj(_C_QDOWN, _C_CKV), qg_ref[...])
    q = _mm(qn, wuq_ref[...])
    q_rope = rope(q[:, GROUP_W:GROUP_W + LANES]) * MLA_SCALE
    lane = lax.broadcasted_iota(jnp.int32, (tm, LANES), 1)
    q_lat = _mm(q[:, 0:GROUP_W], wuk_ref[...]) * MLA_SCALE
    for hh in range(N_HEADS):
        qcat_ref[0, hh, :, 0:KV_RANK] = q_lat[:, hh * KV_RANK:(hh + 1) * KV_RANK].astype(qcat_ref.dtype)
        q_r = q_rope if hh == 0 else pltpu.roll(q_rope, LANES - hh * ROPE_DIM, 1)
        qcat_ref[0, hh, :, KV_RANK:QK_W] = jnp.where(lane < ROPE_DIM, q_r, 0.0).astype(qcat_ref.dtype)
    ckv = _rms(proj(_C_CKV, _C_MLAG), kvg_ref[...])
    krope = rope(proj(_C_KR, _C_END))
    ckv_ref[...] = ckv
    kr_ref[...] = krope[:, 0:ROPE_DIM]
    kcat_ref[:, 0:KV_RANK] = ckv.astype(kcat_ref.dtype)
    kcat_ref[:, KV_RANK:QK_W] = jnp.where(lane == ONES_LANE - KV_RANK, 1.0, krope).astype(kcat_ref.dtype)
    mlag_ref[...] = _silu(proj(_C_MLAG, _C_MEMQ))

    mq = proj(_C_MEMQ, _C_MEMG) * MEM_SCALE
    mg = _silu(proj(_C_MEMG, _C_KR))
    if decode:
        memq_ref[...] = mq
        memg_ref[...] = mg
    else:
        mk, mv = mk_ref[0], mv_ref[0]
        head = lax.broadcasted_iota(jnp.int32, (tm, GROUP_W), 1) // HEAD_DIM
        acc = jnp.zeros((tm, GROUP_W), F32)
        for hh in range(N_HEADS):
            s = _mm_nt(jnp.where(head == hh, mq, 0.0), mk)
            p = jnp.exp(s - jnp.max(s, axis=-1, keepdims=True))
            p = p / jnp.sum(p, axis=-1, keepdims=True)
            acc = acc + jnp.where(head == hh, _mm(p, mv), 0.0)
        mmem_ref[...] = (acc * mg).astype(mmem_ref.dtype)


def _proj_call(x, lw, *, decode, seq_len, pos0=0.0, conv_state=None, mk=None, mv=None):
    m, d = x.shape
    tm = _tile(seq_len if not decode else m, PROJ_ROWS)
    tps = 1 if decode else seq_len // tm
    nb = m // (tm * tps)
    grid = (m // tm,)
    row = lambda w: pl.BlockSpec((tm, w), lambda i: (i, 0))
    consts = [lw['pre_g'], lw['w_in'], lw['conv_w'], lw['q_norm_g'], lw['w_uq'], lw['kv_norm_g'],
              lw['w_ukT'], lw['inv_full']]
    in_specs = [row(d)] + [_const_spec(c.shape) for c in consts]
    qcat_shape = (nb, N_HEADS, tm * tps, QK_W)
    qcat_spec = pl.BlockSpec((1, N_HEADS, tm, QK_W), lambda i: (i // tps, 0, i % tps, 0))
    sds = jax.ShapeDtypeStruct
    common_out = [
        (sds((m, RWKV_SHIFT_W), F32), row(RWKV_SHIFT_W)),
        (sds((m, GROUP_W), F32), row(GROUP_W)),
        (sds((m, KV_RANK), F32), row(KV_RANK)),
        (sds((m, ROPE_DIM), F32), row(ROPE_DIM)),
        (sds((m, QK_W), BF16), row(QK_W)),
        (sds(qcat_shape, BF16), qcat_spec),
        (sds((m, GROUP_W), F32), row(GROUP_W)),
    ]
    if decode:
        ins = [x] + consts + [conv_state[:, 0], conv_state[:, 1]]
        in_specs += [row(GROUP_W), row(GROUP_W)]
        outs = ([(sds((m, GROUP_W), BF16), row(GROUP_W)),
                 (sds((m, GROUP_W), F32), row(GROUP_W)), (sds((m, GROUP_W), F32), row(GROUP_W))]
                + common_out
                + [(sds((m, GROUP_W), F32), row(GROUP_W)), (sds((m, GROUP_W), F32), row(GROUP_W))])
        scratch = []
    else:
        ins = [x] + consts + [mk, mv]
        n_mem = mk.shape[1]
        in_specs += [pl.BlockSpec((1, n_mem, GROUP_W), lambda i: (i // tps, 0, 0))] * 2
        outs = ([(sds((m, GROUP_W), BF16), row(GROUP_W)),
                 (sds((nb, CONV_K - 1, GROUP_W), F32),
                  pl.BlockSpec((1, CONV_K - 1, GROUP_W), lambda i: (i // tps, 0, 0)))]
                + common_out
                + [(sds((m, GROUP_W), BF16), row(GROUP_W))])
        scratch = [pltpu.VMEM((8, GROUP_W), F32), pltpu.VMEM((tm, LANES), F32), pltpu.VMEM((tm, LANES), F32)]
    return pl.pallas_call(
        functools.partial(_proj_body, decode, tps, pos0),
        grid=grid, in_specs=in_specs,
        out_specs=[o[1] for o in outs], out_shape=[o[0] for o in outs],
        scratch_shapes=scratch, compiler_params=_params(("arbitrary",)),
        name="proj_decode" if decode else "proj_prompt",
    )(*ins)


def _memkv_body(x_ref, g_ref, w_ref, k_ref, v_ref, kb_ref, vb_ref):
    kv = _mm(_rms(x_ref[...], g_ref[...]), w_ref[...])
    k, v = kv[:, :GROUP_W], kv[:, GROUP_W:]
    k_ref[...] = k
    v_ref[...] = v
    kb_ref[...] = k.astype(BF16)
    vb_ref[...] = v.astype(BF16)


def _memkv_call(mem, g, w_kv):
    m, d = mem.shape
    tm = _tile(m, PROJ_ROWS)
    row = lambda w: pl.BlockSpec((tm, w), lambda i: (i, 0))
    sds = jax.ShapeDtypeStruct
    return pl.pallas_call(
        _memkv_body, grid=(m // tm,),
        in_specs=[row(d), _const_spec(g.shape), _const_spec(w_kv.shape)],
        out_specs=[row(GROUP_W)] * 4,
        out_shape=[sds((m, GROUP_W), F32), sds((m, GROUP_W), F32), sds((m, GROUP_W), BF16), sds((m, GROUP_W), BF16)],
        compiler_params=_params(("arbitrary",)), name="mem_kv",
    )(mem, g, w_kv)


def _head_ones():
    r = lax.broadcasted_iota(jnp.int32, (GROUP_W, GROUP_W), 0) // HEAD_DIM
    c = lax.broadcasted_iota(jnp.int32, (GROUP_W, GROUP_W), 1) // HEAD_DIM
    return (r == c).astype(F32)


def _rwkv_tokens(mixed, prm, ones_bd):
    w0, a0, k_k, k_a, r_k = (prm[i:i + 1, :] for i in range(5))
    w2p, a2p = prm[8:8 + LANES, :], prm[8 + LANES:8 + 2 * LANES, :]
    r, k, v = mixed[:, 0:GROUP_W], mixed[:, GROUP_W:2 * GROUP_W], mixed[:, 2 * GROUP_W:3 * GROUP_W]
    wa = mixed[:, 3 * GROUP_W:RWKV_SHIFT_W]
    logw = -DECAY_SCALE * _sigmoid(w0 + _mm_hi(jnp.tanh(wa), w2p))
    a = _sigmoid(a0 + _mm_hi(wa, a2p))
    kkr = k * k_k
    kk = kkr * lax.rsqrt(_mm(kkr * kkr, ones_bd) + 1e-12)
    k2 = k * (1.0 + (a - 1.0) * k_a)
    bonus = _mm(r * k2 * r_k, ones_bd) * v
    return r, k2, v, logw, kk, a, bonus


def _rwkv_finish(y, bonus, gate, prm, ones_bd):
    lnx_g, lnx_b = prm[5:6, :], prm[6:7, :]
    mean = _mm(y, ones_bd) * (1.0 / HEAD_DIM)
    yc = y - mean
    var = _mm(yc * yc, ones_bd) * (1.0 / HEAD_DIM)
    return ((yc * lax.rsqrt(var + LNX_EPS)) * lnx_g + lnx_b + bonus) * gate


def _stack_heads(x, head_lane):
    return jnp.concatenate([jnp.where(head_lane == hh, x, 0.0) for hh in range(N_HEADS)], axis=0)


def _unstack_heads(x):
    c = x.shape[0] // N_HEADS
    return x[0:c] + x[c:2 * c] + x[2 * c:3 * c] + x[3 * c:4 * c]


def _rwkv_prompt_body(x_ref, gate_ref, mu_ref, prm_ref, o_ref, st_ref, prev_ref, s_ref):
    ct = x_ref.shape[1]
    si = pl.program_id(1)

    @pl.when(si == 0)
    def _():
        prev_ref[...] = jnp.zeros_like(prev_ref)
        s_ref[...] = jnp.zeros_like(s_ref)

    x = x_ref[0]
    row = lax.broadcasted_iota(jnp.int32, x.shape, 0)
    prev = jnp.where(row == 0, prev_ref[0:1, :], pltpu.roll(x, 1, 0))
    prev_ref[0:1, :] = x[ct - 1:ct, :]
    mixed = x + (prev - x) * mu_ref[...]
    prm = prm_ref[...]
    ones_bd = _head_ones()
    r, k2, v, logw, kk, a, bonus = _rwkv_tokens(mixed, prm, ones_bd)
    b = kk * a

    n = N_HEADS * CHUNK
    head_lane = lax.broadcasted_iota(jnp.int32, (CHUNK, GROUP_W), 1) // HEAD_DIM
    rr = lax.broadcasted_iota(jnp.int32, (n, n), 0)
    cc = lax.broadcasted_iota(jnp.int32, (n, n), 1)
    same_sub = (rr // SUB) == (cc // SUB)
    same_head = (rr // CHUNK) == (cc // CHUNK)
    eye = (rr == cc).astype(F32)
    t_w = lax.broadcasted_iota(jnp.int32, (CHUNK, GROUP_W), 0)
    s_w = lax.broadcasted_iota(jnp.int32, (CHUNK, GROUP_W), 1) % CHUNK
    strict_w, incl_w = t_w > s_w, t_w >= s_w
    eye_w = (t_w == s_w).astype(F32)
    eye_sub = (lax.broadcasted_iota(jnp.int32, (SUB, GROUP_W), 0)
               == lax.broadcasted_iota(jnp.int32, (SUB, GROUP_W), 1) % SUB).astype(F32)
    tri = (lax.broadcasted_iota(jnp.int32, (CHUNK, CHUNK), 0)
           >= lax.broadcasted_iota(jnp.int32, (CHUNK, CHUNK), 1)).astype(F32)

    def sub_square(z):
        return jnp.where(same_sub, jnp.concatenate([z] * (n // SUB), axis=0), 0.0)

    def sub_strip(z):
        out = z[0:SUB]
        for i in range(1, n // SUB):
            out = out + z[i * SUB:(i + 1) * SUB]
        return out

    chunks = range(ct // CHUNK)
    each = lambda f, *xs: [f(*(x[c] for x in xs)) for c in chunks]
    rows = lambda z: [z[c * CHUNK:(c + 1) * CHUNK] for c in chunks]
    stack = lambda z: _stack_heads(z, head_lane)
    lw = rows(logw)
    cw = each(lambda l: _mm_exact_lhs(tri, l), lw)
    cwl = each(lambda z: z[CHUNK - 1:CHUNK, :], cw)
    e_ng = each(lambda z: jnp.exp(-z), cw)
    e_w = each(lambda z, zl: jnp.exp(zl - z), cw, cwl)
    kkd = each(lambda x, z, l: x * jnp.exp(z - l), rows(kk), cw, lw)
    rd = each(lambda x, z: x * jnp.exp(z), rows(r), cw)
    kinvw = each(lambda x, e: x * e, rows(k2), e_w)
    binvw = each(lambda x, e: x * e, rows(b), e_w)
    v_w = rows(v)
    kkd_sq = each(stack, kkd)
    kinv_sq = each(lambda x, e: stack(x * e), rows(k2), e_ng)
    binv_sq = each(lambda x, e: stack(x * e), rows(b), e_ng)
    v_sq = each(stack, v_w)
    kr_rows = each(lambda x, y: jnp.concatenate([x, y], axis=0), kkd, rd)
    ab = each(_mm_nt, kr_rows, binv_sq)
    ak = each(_mm_nt, kr_rows, kinv_sq)
    a_bk = each(lambda z: jnp.where(strict_w, z[0:CHUNK], 0.0), ab)
    m_rb = each(lambda z: jnp.where(incl_w, z[CHUNK:2 * CHUNK], 0.0), ab)
    a_vk = each(lambda z: jnp.where(strict_w, z[0:CHUNK], 0.0), ak)
    m_rk = each(lambda z: jnp.where(incl_w, z[CHUNK:2 * CHUNK], 0.0), ak)
    a_sq = each(stack, a_bk)
    d1_sq = each(lambda x: jnp.where(same_sub, x, 0.0), a_sq)
    e1_sq = each(lambda x, y: x - y, a_sq, d1_sq)
    d1 = each(sub_strip, d1_sq)
    d2 = each(_mm, d1, d1_sq)
    t0 = each(lambda x: eye_sub - x, d1)
    r2 = each(lambda x, y, z: _mm(jnp.concatenate([x, y], axis=0), sub_square(z)), d2, t0, d2)
    d4 = each(lambda z: z[0:SUB], r2)
    t1 = each(lambda x, z: x + z[SUB:2 * SUB], t0, r2)
    r4 = each(lambda x, y, z: _mm(jnp.concatenate([x, y], axis=0), sub_square(z)), d4, t1, d4)
    d8 = each(lambda z: z[0:SUB], r4)
    t2 = each(lambda x, z: x + z[SUB:2 * SUB], t1, r4)
    dinv = each(lambda x, y: x + _mm(x, sub_square(y)), t2, d8)
    dinv_sq = each(sub_square, dinv)
    dinv_w = each(_unstack_heads, dinv_sq)
    nn = each(_mm, dinv_w, e1_sq)
    nn_sq = each(stack, nn)
    n2 = each(_mm, nn, nn_sq)
    t3 = each(lambda x, y: _mm(eye_w - x, eye + stack(y)), nn, n2)
    tinv = each(_mm, t3, dinv_sq)
    xv = each(lambda x, y, z: _mm(jnp.concatenate([x, y], axis=0), z), a_vk, m_rk, v_sq)
    x_w = each(lambda z: z[0:CHUNK], xv)
    kkdp = each(_mm, tinv, kkd_sq)
    vp = each(_mm, tinv, each(stack, x_w))
    kkdp_sq = each(stack, kkdp)
    rq = each(lambda x, y, z: x - _mm(y, z), rd, m_rb, kkdp_sq)
    y_in = each(lambda z, a3, a4: z[CHUNK:2 * CHUNK] - _mm(a3, stack(a4)), xv, m_rb, vp)
    phi = each(lambda zl, x, y: eye * jnp.exp(zl) - jnp.where(same_head, _mm_tn(x, y), 0.0), cwl, kkdp, binvw)
    g = each(lambda a1, a2, a3, a4: _unstack_heads(jnp.where(
        same_head, _mm_tn(jnp.concatenate([a1, -a3], axis=0), jnp.concatenate([a2, a4], axis=0)), 0.0)),
        v_w, kinvw, vp, binvw)
    state = s_ref[...]
    ys = []
    for c in chunks:
        ys.append(_mm_nt(rq[c], stack(state)) + y_in[c])
        state = _mm(state, phi[c]) + g[c]
    s_ref[...] = state
    st_ref[0] = state
    y = jnp.concatenate(ys, axis=0) if len(ys) > 1 else ys[0]
    o_ref[...] = _rwkv_finish(y, bonus, gate_ref[...], prm, ones_bd).astype(o_ref.dtype)


def _rwkv_prompt_call(r_sh, gate, lw, nb, seq_len):
    ct = _tile(seq_len, RWKV_ROWS)
    ns = seq_len // ct
    x3 = r_sh.reshape(nb, seq_len, RWKV_SHIFT_W)
    sds = jax.ShapeDtypeStruct
    out, st = pl.pallas_call(
        _rwkv_prompt_body, grid=(nb, ns),
        in_specs=[pl.BlockSpec((1, ct, RWKV_SHIFT_W), lambda b, s: (b, s, 0)),
                  pl.BlockSpec((ct, GROUP_W), lambda b, s: (b * ns + s, 0)),
                  _const_spec(lw['mu'].shape), _const_spec(lw['rwkv_prm'].shape)],
        out_specs=[pl.BlockSpec((ct, GROUP_W), lambda b, s: (b * ns + s, 0)),
                   pl.BlockSpec((1, HEAD_DIM, GROUP_W), lambda b, s: (b, 0, 0))],
        out_shape=[sds((nb * seq_len, GROUP_W), BF16), sds((nb, HEAD_DIM, GROUP_W), F32)],
        scratch_shapes=[pltpu.VMEM((8, RWKV_SHIFT_W), F32),
                        pltpu.VMEM((HEAD_DIM, GROUP_W), F32)],
        compiler_params=_params(("arbitrary", "arbitrary")), name="rwkv_prompt",
    )(x3, gate, lw['mu'], lw['rwkv_prm'])
    st = st.reshape(nb, HEAD_DIM, N_HEADS, HEAD_DIM).transpose(0, 2, 1, 3)
    return out, st


def _rwkv_step_body(x_ref, prev_ref, gate_ref, mu_ref, prm_ref, s_ref, o_ref, so_ref, yt_ref, vt_ref):
    hh = pl.program_id(0)
    x = x_ref[...]
    mixed = x + (prev_ref[...] - x) * mu_ref[...]
    prm = prm_ref[...]
    ones_bd = _head_ones()
    r, k2, v, logw, kk, a, bonus = _rwkv_tokens(mixed, prm, ones_bd)
    row0 = pl.multiple_of(hh * HEAD_DIM, HEAD_DIM)
    for i, z in enumerate((kk, jnp.exp(logw), kk * a, v, k2, r)):
        vt_ref[i] = z.T
    head = lambda i: vt_ref[i, pl.ds(row0, HEAD_DIM), :]
    st = s_ref[...]
    sa = jnp.sum(st * head(0)[None, :, :], axis=1)
    st = st * head(1)[None, :, :] - sa[:, None, :] * head(2)[None, :, :] + head(3)[:, None, :] * head(4)[None, :, :]
    so_ref[...] = st
    yt_ref[pl.ds(row0, HEAD_DIM), :] = jnp.sum(st * head(5)[None, :, :], axis=1)

    @pl.when(hh == N_HEADS - 1)
    def _():
        o_ref[...] = _rwkv_finish(yt_ref[...].T, bonus, gate_ref[...], prm, ones_bd).astype(o_ref.dtype)


def _rwkv_step_call(layer, r_sh, prev, gate, state_t, lw):
    m = r_sh.shape[0]
    full = lambda w: pl.BlockSpec((m, w), lambda h: (0, 0))
    sds = jax.ShapeDtypeStruct
    return pl.pallas_call(
        _rwkv_step_body, grid=(N_HEADS,),
        in_specs=[full(RWKV_SHIFT_W), full(RWKV_SHIFT_W), full(GROUP_W),
                  _const_spec(lw['mu'].shape), _const_spec(lw['rwkv_prm'].shape),
                  pl.BlockSpec((None, None, HEAD_DIM, HEAD_DIM, m), lambda h: (layer, h, 0, 0, 0))],
        out_specs=[full(GROUP_W), pl.BlockSpec((None, HEAD_DIM, HEAD_DIM, m), lambda h: (h, 0, 0, 0))],
        out_shape=[sds((m, GROUP_W), BF16), sds((N_HEADS, HEAD_DIM, HEAD_DIM, m), F32)],
        scratch_shapes=[pltpu.VMEM((GROUP_W, m), F32), pltpu.VMEM((6, GROUP_W, m), F32)],
        compiler_params=_params(("arbitrary",)), name="rwkv_step",
    )(r_sh, prev, gate, lw['mu'], lw['rwkv_prm'], state_t)


def _softmax_update(s, v, m_sc, l_sc, acc_sc):
    m_prev = m_sc[...]
    m_new = jnp.maximum(m_prev, jnp.max(s, axis=-1, keepdims=True))
    alpha = jnp.exp(m_prev - m_new)
    p = jnp.exp(s - m_new)
    l_sc[...] = alpha * l_sc[...] + jnp.sum(p, axis=-1, keepdims=True)
    acc_sc[...] = alpha * acc_sc[...] + _mm(p, v)
    m_sc[...] = m_new


def _flash_body(q_ref, k_ref, gate_ref, wuv_ref, x_ref, mconv_ref, mrwkv_ref, mmem_ref, wout_ref, pg_ref,
                o_ref, sa_ref, sb_ref, m_ref, acc_ref):
    qi = pl.program_id(1)
    tq = q_ref.shape[2]

    def keys(kb):
        return k_ref[0, pl.ds(pl.multiple_of(kb * tq, tq), tq), :]

    def scores(kb, s_ref):
        k = keys(kb)
        for hh in range(N_HEADS):
            s_ref[hh] = _mm_nt(q_ref[0, hh], k)

    def update(s_ref, kb, masked):
        k = keys(kb)
        for hh in range(N_HEADS):
            s_h = s_ref[hh]
            if masked:
                s_h = jnp.where(lax.broadcasted_iota(jnp.int32, s_h.shape, 1)
                                <= lax.broadcasted_iota(jnp.int32, s_h.shape, 0), s_h, NEG_INF)
            m_prev = m_ref[hh]
            m_new = jnp.maximum(m_prev, jnp.max(s_h, axis=-1, keepdims=True))
            p = jnp.exp(s_h - m_new)
            acc_ref[hh] = jnp.exp(m_prev - m_new) * acc_ref[hh] + _mm(p, k)
            m_ref[hh] = m_new

    def finish(s_ref):
        update(s_ref, qi, True)
        mla = jnp.zeros((tq, GROUP_W), F32)
        for hh in range(N_HEADS):
            acc = acc_ref[hh]
            o = acc[:, 0:KV_RANK] / acc[:, ONES_LANE:ONES_LANE + 1]
            mla = mla + _mm(o, wuv_ref[hh])
        mixed = (mconv_ref[...], mrwkv_ref[...], (mla * gate_ref[...]).astype(BF16), mmem_ref[...])
        y = jnp.zeros(x_ref.shape, F32)
        for i, m_i in enumerate(mixed):
            y = y + _mm(m_i, wout_ref[i * GROUP_W:(i + 1) * GROUP_W, :])
        o_ref[...] = x_ref[...] + _rms(y, pg_ref[...])

    m_ref[...] = jnp.full_like(m_ref, NEG_INF)
    acc_ref[...] = jnp.zeros_like(acc_ref)
    scores(0, sa_ref)

    def pair(j, _):
        kb = 2 * j
        scores(kb + 1, sb_ref)
        update(sa_ref, kb, False)
        scores(kb + 2, sa_ref)
        update(sb_ref, kb + 1, False)
        return 0

    lax.fori_loop(0, qi // 2, pair, 0)

    @pl.when(qi % 2 == 1)
    def _():
        scores(qi, sb_ref)
        update(sa_ref, qi - 1, False)
        finish(sb_ref)

    @pl.when(qi % 2 == 0)
    def _():
        finish(sa_ref)


def _flash_call(qcat, kcat, gate, w_uv_pad, x, m_conv, m_rwkv, m_mem, w_out, post_g, seq_len):
    nb = qcat.shape[0]
    d = x.shape[1]
    tq = _tile(seq_len, FLASH_ROWS)
    nq = seq_len // tq
    k3 = kcat.reshape(nb, seq_len, QK_W)
    row = lambda w: pl.BlockSpec((tq, w), lambda b, i: (b * nq + i, 0))
    return pl.pallas_call(
        _flash_body, grid=(nb, nq),
        in_specs=[pl.BlockSpec((1, N_HEADS, tq, QK_W), lambda b, i: (b, 0, i, 0)),
                  pl.BlockSpec((1, seq_len, QK_W), lambda b, i: (b, 0, 0)),
                  row(GROUP_W), _const_spec(w_uv_pad.shape),
                  row(d), row(GROUP_W), row(GROUP_W), row(GROUP_W),
                  _const_spec(w_out.shape), _const_spec(post_g.shape)],
        out_specs=row(d),
        out_shape=jax.ShapeDtypeStruct((nb * seq_len, d), F32),
        scratch_shapes=[pltpu.VMEM((N_HEADS, tq, tq), F32), pltpu.VMEM((N_HEADS, tq, tq), F32),
                        pltpu.VMEM((N_HEADS, tq, 1), F32), pltpu.VMEM((N_HEADS, tq, QK_W), F32)],
        compiler_params=_params(("arbitrary", "arbitrary")), name="flash_mla",
    )(qcat, k3, gate, w_uv_pad, x, m_conv, m_rwkv, m_mem, w_out, post_g)


def _decode_body(layer, nch, pg, pt_ref, q_ref, cnew_ref, knew_ref, gate_ref, wuv_ref, ckv_hbm, kr_hbm,
                 o_ref, cbuf, kbuf, sem, m_sc, l_sc, acc_sc):
    b, c = pl.program_id(0), pl.program_id(1)
    step = b * nch + c
    total = pl.num_programs(0) * nch
    slot = step % 2
    page_len = cbuf.shape[2]

    def page_copies(st, sl):
        bb, cc = st // nch, st % nch
        out = []
        for p in range(pg):
            page = pt_ref[bb, cc * pg + p]
            out.append(pltpu.make_async_copy(ckv_hbm.at[layer, page], cbuf.at[sl, p], sem.at[0, sl]))
            out.append(pltpu.make_async_copy(kr_hbm.at[layer, page], kbuf.at[sl, :, pl.ds(p * page_len, page_len)],
                                             sem.at[1, sl]))
        return out

    @pl.when(step == 0)
    def _():
        for cp in page_copies(0, 0):
            cp.start()

    @pl.when(step + 1 < total)
    def _():
        for cp in page_copies(step + 1, 1 - slot):
            cp.start()

    @pl.when(c == 0)
    def _():
        m_sc[...] = jnp.full_like(m_sc, NEG_INF)
        l_sc[...] = jnp.zeros_like(l_sc)
        acc_sc[...] = jnp.zeros_like(acc_sc)

    for cp in page_copies(step, slot):
        cp.wait()

    ck = cbuf[slot].reshape(pg * page_len, KV_RANK).astype(BF16)
    q = q_ref[0]
    s = _mm_nt(q[:, 0:KV_RANK], ck) + _mm(q[:, KV_RANK:QK_DIM], kbuf[slot])
    _softmax_update(s, ck, m_sc, l_sc, acc_sc)

    @pl.when(c == nch - 1)
    def _():
        qf = q.astype(F32)
        cn, kn = cnew_ref[0], knew_ref[0]
        s_new = (jnp.sum(qf[:, 0:KV_RANK] * cn, axis=-1, keepdims=True)
                 + jnp.sum(qf[:, KV_RANK:QK_DIM] * kn, axis=-1, keepdims=True))
        m_prev = m_sc[...]
        m_new = jnp.maximum(m_prev, s_new)
        alpha, p_new = jnp.exp(m_prev - m_new), jnp.exp(s_new - m_new)
        l_fin = alpha * l_sc[...] + p_new
        o = (alpha * acc_sc[...] + p_new * cn) / l_fin
        out = jnp.zeros((1, GROUP_W), F32)
        for hh in range(N_HEADS):
            out = out + _mm(o, wuv_ref[hh])[hh:hh + 1, :]
        o_ref[0] = out * gate_ref[0]


def _decode_call(layer, page_table, q16, ckv_new, kr_new, gate, w_uv_pad, cache_ckv, cache_krope):
    nb, n_pages = page_table.shape
    page = cache_ckv.shape[2]
    pg = _tile(n_pages, DECODE_PAGES)
    nch = n_pages // pg
    blk = lambda shape: pl.BlockSpec((1,) + shape, lambda b, c, pt: (b, 0, 0))
    grid_spec = pltpu.PrefetchScalarGridSpec(
        num_scalar_prefetch=1, grid=(nb, nch),
        in_specs=[blk((16, QK_W)), blk((1, KV_RANK)), blk((1, ROPE_DIM)), blk((1, GROUP_W)),
                  pl.BlockSpec(w_uv_pad.shape, lambda b, c, pt: (0, 0, 0)),
                  pl.BlockSpec(memory_space=pl.ANY), pl.BlockSpec(memory_space=pl.ANY)],
        out_specs=blk((1, GROUP_W)),
        scratch_shapes=[pltpu.VMEM((2, pg, page, KV_RANK), F32), pltpu.VMEM((2, ROPE_DIM, pg * page), F32),
                        pltpu.SemaphoreType.DMA((2, 2)),
                        pltpu.VMEM((16, 1), F32), pltpu.VMEM((16, 1), F32), pltpu.VMEM((16, KV_RANK), F32)])
    out = pl.pallas_call(
        functools.partial(_decode_body, layer, nch, pg), grid_spec=grid_spec,
        out_shape=jax.ShapeDtypeStruct((nb, 1, GROUP_W), F32),
        compiler_params=_params(("arbitrary", "arbitrary")), name="paged_decode",
    )(page_table, q16, ckv_new[:, None, :], kr_new[:, None, :], gate[:, None, :], w_uv_pad,
      cache_ckv, cache_krope)
    return out.reshape(nb, GROUP_W)


def _memattn_body(q_ref, gate_ref, k_ref, v_ref, o_ref):
    tb = q_ref.shape[0]
    head = lax.broadcasted_iota(jnp.int32, (8, GROUP_W), 1) // HEAD_DIM
    sel = head == lax.broadcasted_iota(jnp.int32, (8, GROUP_W), 0)
    for bi in range(tb):
        q_bd = jnp.where(sel, q_ref[bi], 0.0)
        s = _mm(q_bd, k_ref[0, bi])
        p = jnp.exp(s - jnp.max(s, axis=-1, keepdims=True))
        p = p / jnp.sum(p, axis=-1, keepdims=True)
        o = jnp.where(sel, _mm_nt(p, v_ref[0, bi]), 0.0)
        o_ref[bi] = jnp.sum(o, axis=0, keepdims=True) * gate_ref[bi]


def _memattn_call(layer, mem_q, gate, mem_k, mem_v):
    m = mem_q.shape[0]
    n_mem = mem_k.shape[3]
    tb = _tile(m, MEMATTN_TOKENS)
    vec = pl.BlockSpec((tb, 1, GROUP_W), lambda i: (i, 0, 0))
    kv = pl.BlockSpec((1, tb, GROUP_W, n_mem), lambda i: (layer, i, 0, 0))
    out = pl.pallas_call(
        _memattn_body, grid=(m // tb,),
        in_specs=[vec, vec, kv, kv], out_specs=vec,
        out_shape=jax.ShapeDtypeStruct((m, 1, GROUP_W), F32),
        compiler_params=_params(("arbitrary",)), name="mem_attn_decode",
    )(mem_q[:, None, :], gate[:, None, :], mem_k, mem_v)
    return out.reshape(m, GROUP_W)


def _outproj_body(x_ref, m0_ref, m1_ref, m2_ref, m3_ref, w_ref, g_ref, o_ref):
    acc = jnp.zeros(x_ref.shape, F32)
    for i, m_ref in enumerate((m0_ref, m1_ref, m2_ref, m3_ref)):
        acc = acc + _mm(m_ref[...], w_ref[i * GROUP_W:(i + 1) * GROUP_W, :])
    o_ref[...] = x_ref[...] + _rms(acc, g_ref[...])


def _outproj_call(x, mixed, w_out, g):
    m, d = x.shape
    tm = _tile(m, PROJ_ROWS)
    row = lambda w: pl.BlockSpec((tm, w), lambda i: (i, 0))
    return pl.pallas_call(
        _outproj_body, grid=(m // tm,),
        in_specs=[row(d)] + [row(GROUP_W)] * 4 + [_const_spec(w_out.shape), _const_spec(g.shape)],
        out_specs=row(d), out_shape=jax.ShapeDtypeStruct((m, d), F32),
        compiler_params=_params(("arbitrary",)), name="out_proj",
    )(x, *mixed, w_out, g)


def _prep_layer(l, pre_norm_g, post_norm_g, w_in, w_out, conv_w, rwkv_mu, rwkv_w0, rwkv_w2, rwkv_a0, rwkv_a2,
                rwkv_k_k, rwkv_k_a, rwkv_r_k, rwkv_lnx_g, rwkv_lnx_b, mla_q_norm_g, mla_w_uq, mla_kv_norm_g,
                mla_w_uk, mla_w_uv, mem_norm_g, w_mem_k, w_mem_v):
    d = w_in.shape[1]
    kr0 = _C_MLAG
    w = w_in[l]
    w_in_p = jnp.concatenate([w[:, :kr0], w[:, kr0 + ROPE_DIM:], w[:, kr0:kr0 + ROPE_DIM],
                              jnp.zeros((d, LANES - ROPE_DIM), F32)], axis=1).astype(BF16)
    uq = mla_w_uq[l].reshape(Q_RANK, N_HEADS, NOPE_DIM + ROPE_DIM)
    w_uq = jnp.concatenate([uq[:, :, :NOPE_DIM].reshape(Q_RANK, -1), uq[:, :, NOPE_DIM:].reshape(Q_RANK, -1)],
                           axis=1).astype(BF16)
    w_uv_pad = jnp.zeros((N_HEADS, KV_RANK, GROUP_W), F32)
    for hh in range(N_HEADS):
        w_uv_pad = w_uv_pad.at[hh, :, hh * HEAD_DIM:(hh + 1) * HEAD_DIM].set(mla_w_uv[l][:, hh, :])
    w_uk_bd = jnp.zeros((N_HEADS * NOPE_DIM, N_HEADS * KV_RANK), F32)
    for hh in range(N_HEADS):
        w_uk_bd = w_uk_bd.at[hh * NOPE_DIM:(hh + 1) * NOPE_DIM, hh * KV_RANK:(hh + 1) * KV_RANK].set(
            mla_w_uk[l][:, hh, :].T)
    half = ROPE_DIM // 2
    inv = jnp.power(ROPE_BASE, -jnp.arange(half, dtype=F32) / half)
    zl = jnp.zeros((LORA_W, GROUP_W), F32)
    vecs = jnp.stack([rwkv_w0[l], rwkv_a0[l], rwkv_k_k[l], rwkv_k_a[l], rwkv_r_k[l].reshape(-1),
                      rwkv_lnx_g[l], rwkv_lnx_b[l], jnp.zeros((GROUP_W,), F32)])
    return {
        'pre_g': pre_norm_g[l][None], 'post_g': post_norm_g[l][None], 'w_in': w_in_p,
        'w_out': w_out[l].astype(BF16), 'conv_w': conv_w[l], 'q_norm_g': mla_q_norm_g[l][None], 'w_uq': w_uq,
        'kv_norm_g': mla_kv_norm_g[l][None], 'w_ukT': w_uk_bd.astype(BF16),
        'w_uv_pad': w_uv_pad.astype(BF16), 'inv_full': jnp.tile(inv, LANES // half)[None],
        'mu': rwkv_mu[l][None],
        'rwkv_prm': jnp.concatenate([vecs, rwkv_w2[l], zl, zl, rwkv_a2[l]], axis=0),
        'mem_g': mem_norm_g[l][None],
        'w_mem_kv': jnp.concatenate([w_mem_k[l], w_mem_v[l]], axis=1).astype(BF16),
    }


def kernel(x_prompt, x_sample, cache_ckv, cache_krope, cache_mem_k, cache_mem_v, state_conv, state_rwkv_shift, state_rwkv, page_table, mem_prompt, pre_norm_g, post_norm_g, w_in, w_out, conv_w, rwkv_mu, rwkv_w0, rwkv_w2, rwkv_a0, rwkv_a2, rwkv_k_k, rwkv_k_a, rwkv_r_k, rwkv_lnx_g, rwkv_lnx_b, mla_q_norm_g, mla_w_uq, mla_kv_norm_g, mla_w_uk, mla_w_uv, mem_norm_g, w_mem_k, w_mem_v):
    bp, sp, d = x_prompt.shape
    bs, ts, _ = x_sample.shape
    assert ts == 1
    depth = w_in.shape[0]
    n_mem = mem_prompt.shape[1]
    past_len = page_table.shape[1] * cache_ckv.shape[2]
    mem_k4 = jnp.transpose(cache_mem_k, (0, 1, 3, 4, 2)).reshape(depth, bs, GROUP_W, n_mem)
    mem_v4 = jnp.transpose(cache_mem_v, (0, 1, 3, 4, 2)).reshape(depth, bs, GROUP_W, n_mem)
    cache_krope_t = jnp.swapaxes(cache_krope, 2, 3)
    state_rwkv_t = jnp.transpose(state_rwkv, (0, 2, 3, 4, 1))
    xp = x_prompt.reshape(bp * sp, d)
    xs = x_sample.reshape(bs, d)
    mem2 = mem_prompt.reshape(bp * n_mem, d)
    outs = [[] for _ in range(12)]
    for l in range(depth):
        lw = _prep_layer(l, pre_norm_g, post_norm_g, w_in, w_out, conv_w, rwkv_mu, rwkv_w0, rwkv_w2, rwkv_a0,
                         rwkv_a2, rwkv_k_k, rwkv_k_a, rwkv_r_k, rwkv_lnx_g, rwkv_lnx_b, mla_q_norm_g, mla_w_uq,
                         mla_kv_norm_g, mla_w_uk, mla_w_uv, mem_norm_g, w_mem_k, w_mem_v)
        mk, mv, mk_b, mv_b = _memkv_call(mem2, lw['mem_g'], lw['w_mem_kv'])
        (m_conv, conv_p, r_sh, r_gate, ckv, kr, kcat, qcat, mla_gate, m_mem) = _proj_call(
            xp, lw, decode=False, seq_len=sp,
            mk=mk_b.reshape(bp, n_mem, GROUP_W), mv=mv_b.reshape(bp, n_mem, GROUP_W))
        m_rwkv, st_p = _rwkv_prompt_call(r_sh, r_gate, lw, bp, sp)
        xp = _flash_call(qcat, kcat, mla_gate, lw['w_uv_pad'], xp, m_conv, m_rwkv, m_mem, lw['w_out'],
                         lw['post_g'], sp)
        sh_p = r_sh.reshape(bp, sp, RWKV_SHIFT_W)[:, -1]
        (s_conv, cn0, cn1, r_sh_s, r_gate_s, ckv_s, kr_s, _, qcat_s, mla_gate_s, mem_q_s, mem_gate_s) = _proj_call(
            xs, lw, decode=True, seq_len=1, pos0=float(past_len), conv_state=state_conv[l])
        s_rwkv, st_s = _rwkv_step_call(l, r_sh_s, state_rwkv_shift[l], r_gate_s, state_rwkv_t, lw)
        q16 = jnp.pad(jnp.transpose(qcat_s[0], (1, 0, 2)), ((0, 0), (0, 16 - N_HEADS), (0, 0)))
        s_mla = _decode_call(l, page_table, q16, ckv_s, kr_s, mla_gate_s, lw['w_uv_pad'], cache_ckv, cache_krope_t)
        s_mem = _memattn_call(l, mem_q_s, mem_gate_s, mem_k4, mem_v4)
        xs = _outproj_call(xs, (s_conv, s_rwkv, s_mla, s_mem), lw['w_out'], lw['post_g'])
        vals = (ckv.reshape(bp, sp, KV_RANK), ckv_s.reshape(bs, ts, KV_RANK),
                kr.reshape(bp, sp, ROPE_DIM), kr_s.reshape(bs, ts, ROPE_DIM),
                mk.reshape(bp, n_mem, N_HEADS, HEAD_DIM), mv.reshape(bp, n_mem, N_HEADS, HEAD_DIM),
                conv_p, jnp.stack([cn0, cn1], axis=1), sh_p, r_sh_s, st_p, st_s)
        for o, v in zip(outs, vals):
            o.append(v)
    res = [jnp.stack(o) for o in outs]
    res[-1] = jnp.transpose(res[-1], (0, 4, 1, 2, 3))
    return (xp.reshape(bp, sp, d), xs.reshape(bs, ts, d)) + tuple(res)
```

```python
import functools

import jax
import jax.numpy as jnp
import numpy as np
from jax import lax
from jax.experimental import pallas as pl
from jax.experimental.pallas import tpu as pltpu

F32 = jnp.float32
BF16 = jnp.bfloat16

HEAD_DIM = 64
N_HEADS = 4
GROUP_W = N_HEADS * HEAD_DIM
CONV_K = 3
LORA_W = 64
RWKV_SHIFT_W = 3 * GROUP_W + 2 * LORA_W
Q_RANK = 256
KV_RANK = 128
NOPE_DIM = 64
ROPE_DIM = 32
QK_DIM = KV_RANK + ROPE_DIM
QK_W = 2 * KV_RANK
ONES_LANE = QK_DIM
RMS_EPS = 1e-6
LNX_EPS = 64e-5
ROPE_BASE = 10000.0
DECAY_SCALE = 0.6065306597
MLA_SCALE = (NOPE_DIM + ROPE_DIM) ** -0.5
MEM_SCALE = HEAD_DIM ** -0.5
NEG_INF = -1e30

LANES = 128
CHUNK = 64
SUB = 16
VMEM_LIMIT = 56 * 1024 * 1024
PROJ_ROWS = 512
RWKV_ROWS = 512
FLASH_ROWS = 512
DECODE_PAGES = 128
MEMATTN_TOKENS = 8

_C_CONV, _C_RSH, _C_RGATE, _C_QDOWN, _C_CKV, _C_MLAG, _C_MEMQ, _C_MEMG, _C_KR, _C_END = (
    0, 1024, 1920, 2176, 2432, 2560, 2816, 3072, 3328, 3456)


def _mm(a, b):
    return jnp.dot(a.astype(BF16), b.astype(BF16), preferred_element_type=F32)


def _mm_nt(a, b):
    return lax.dot_general(a.astype(BF16), b.astype(BF16), (((1,), (1,)), ((), ())),
                           preferred_element_type=F32)


def _mm_tn(a, b):
    return lax.dot_general(a.astype(BF16), b.astype(BF16), (((0,), (0,)), ((), ())),
                           preferred_element_type=F32)


def _split_bf16(x, n):
    parts = []
    for _ in range(n):
        p = x.astype(BF16)
        parts.append(p)
        x = x - p.astype(F32)
    return parts


def _mm_hi(a, b):
    (a1, a2), (b1, b2) = _split_bf16(a, 2), _split_bf16(b, 2)
    dot = lambda x, y: jnp.dot(x, y, preferred_element_type=F32)
    return dot(a1, b1) + (dot(a1, b2) + dot(a2, b1))


def _mm_exact_lhs(a01, b):
    a = a01.astype(BF16)
    b1, b2, b3 = _split_bf16(b, 3)
    dot = lambda x: jnp.dot(a, x, preferred_element_type=F32)
    return dot(b1) + (dot(b2) + dot(b3))


def _sigmoid(x):
    return 1.0 / (1.0 + jnp.exp(-x))


def _silu(x):
    return x * _sigmoid(x)


def _rms(x, g):
    return x * lax.rsqrt(jnp.mean(x * x, axis=-1, keepdims=True) + RMS_EPS) * g


def _params(sem):
    return pltpu.CompilerParams(dimension_semantics=sem, vmem_limit_bytes=VMEM_LIMIT)


def _const_spec(shape):
    nd = len(shape)
    return pl.BlockSpec(shape, lambda *_: (0,) * nd)


def _tile(n, pref):
    t = min(n, pref)
    assert n % t == 0, (n, pref)
    return t


def _proj_body(decode, tiles_per_seq, pos0, *refs):
    if decode:
        (x_ref, g_ref, w_ref, cw_ref, qg_ref, wuq_ref, kvg_ref, wuk_ref, inv_ref, cb0_ref, cb1_ref,
         mconv_ref, cn0_ref, cn1_ref, rsh_ref, rgate_ref, ckv_ref, kr_ref, kcat_ref, qcat_ref,
         mlag_ref, memq_ref, memg_ref) = refs
    else:
        (x_ref, g_ref, w_ref, cw_ref, qg_ref, wuq_ref, kvg_ref, wuk_ref, inv_ref, mk_ref, mv_ref,
         mconv_ref, cst_ref, rsh_ref, rgate_ref, ckv_ref, kr_ref, kcat_ref, qcat_ref,
         mlag_ref, mmem_ref, carry_ref, cos_row_ref, sin_row_ref) = refs
    tm = x_ref.shape[0]
    t = pl.program_id(0) % tiles_per_seq
    h = _rms(x_ref[...], g_ref[...]).astype(BF16)

    def proj(lo, hi):
        return jnp.dot(h, w_ref[:, lo:hi], preferred_element_type=F32)

    cv = proj(_C_CONV, _C_RSH)
    c_b, c_c, c_x, c_g = (cv[:, i * GROUP_W:(i + 1) * GROUP_W] for i in range(4))
    u = c_c * c_x
    w0, w1, w2 = cw_ref[0:1, :], cw_ref[1:2, :], cw_ref[2:3, :]
    if decode:
        b0, b1 = cb0_ref[...], cb1_ref[...]
        y = b0 * w0 + b1 * w1 + u * w2
        cn0_ref[...] = b1
        cn1_ref[...] = u
    else:
        @pl.when(t == 0)
        def _():
            carry_ref[...] = jnp.zeros_like(carry_ref)
        p0, p1 = carry_ref[0:1, :], carry_ref[1:2, :]
        row = lax.broadcasted_iota(jnp.int32, u.shape, 0)
        u1 = jnp.where(row == 0, p1, pltpu.roll(u, 1, 0))
        u2 = jnp.where(row == 0, p0, jnp.where(row == 1, p1, pltpu.roll(u, 2, 0)))
        y = u2 * w0 + u1 * w1 + u * w2
        carry_ref[0:2, :] = u[tm - 2:tm, :]
        cst_ref[0] = u[tm - 2:tm, :]
    mconv_ref[...] = (c_b * y * _silu(c_g)).astype(mconv_ref.dtype)

    rsh_ref[...] = proj(_C_RSH, _C_RGATE)
    rgate_ref[...] = _silu(proj(_C_RGATE, _C_QDOWN))

    if decode:
        ang = jnp.full((tm, 1), pos0, F32) * inv_ref[...]
        cos, sin = jnp.cos(ang), jnp.sin(ang)
    else:
        @pl.when(pl.program_id(0) == 0)
        def _():
            ang_row = lax.broadcasted_iota(jnp.int32, (tm, 1), 0).astype(F32) * inv_ref[...]
            cos_row_ref[...] = jnp.cos(ang_row)
            sin_row_ref[...] = jnp.sin(ang_row)
        ang_tile = (t * tm).astype(F32) * inv_ref[...]
        cos_t, sin_t = jnp.cos(ang_tile), jnp.sin(ang_tile)
        cos_r, sin_r = cos_row_ref[...], sin_row_ref[...]
        cos = cos_t * cos_r - sin_t * sin_r
        sin = sin_t * cos_r + cos_t * sin_r
    first = (lax.broadcasted_iota(jnp.int32, (tm, LANES), 1) % ROPE_DIM) < (ROPE_DIM // 2)

    def rope(v):
        partner = jnp.where(first, -pltpu.roll(v, LANES - ROPE_DIM // 2, 1), pltpu.roll(v, ROPE_DIM // 2, 1))
        return v * cos + partner * sin

    qn = _rms(proj(_C_QDOWN, _C_CKV), qg_ref[...])
    q = _mm(qn, wuq_ref[...])
    q_rope = rope(q[:, GROUP_W:GROUP_W + LANES]) * MLA_SCALE
    lane = lax.broadcasted_iota(jnp.int32, (tm, LANES), 1)
    q_lat = _mm(q[:, 0:GROUP_W], wuk_ref[...]) * MLA_SCALE
    for hh in range(N_HEADS):
        qcat_ref[0, hh, :, 0:KV_RANK] = q_lat[:, hh * KV_RANK:(hh + 1) * KV_RANK].astype(qcat_ref.dtype)
        q_r = q_rope if hh == 0 else pltpu.roll(q_rope, LANES - hh * ROPE_DIM, 1)
        qcat_ref[0, hh, :, KV_RANK:QK_W] = jnp.where(lane < ROPE_DIM, q_r, 0.0).astype(qcat_ref.dtype)
    ckv = _rms(proj(_C_CKV, _C_MLAG), kvg_ref[...])
    krope = rope(proj(_C_KR, _C_END))
    ckv_ref[...] = ckv
    kr_ref[...] = krope[:, 0:ROPE_DIM]
    kcat_ref[:, 0:KV_RANK] = ckv.astype(kcat_ref.dtype)
    kcat_ref[:, KV_RANK:QK_W] = jnp.where(lane == ONES_LANE - KV_RANK, 1.0, krope).astype(kcat_ref.dtype)
    mlag_ref[...] = _silu(proj(_C_MLAG, _C_MEMQ))

    mq = proj(_C_MEMQ, _C_MEMG) * MEM_SCALE
    mg = _silu(proj(_C_MEMG, _C_KR))
    if decode:
        memq_ref[...] = mq
        memg_ref[...] = mg
    else:
        mk, mv = mk_ref[0], mv_ref[0]
        head = lax.broadcasted_iota(jnp.int32, (tm, GROUP_W), 1) // HEAD_DIM
        acc = jnp.zeros((tm, GROUP_W), F32)
        for hh in range(N_HEADS):
            s = _mm_nt(jnp.where(head == hh, mq, 0.0), mk)
            p = jnp.exp(s - jnp.max(s, axis=-1, keepdims=True))
            p = p / jnp.sum(p, axis=-1, keepdims=True)
            acc = acc + jnp.where(head == hh, _mm(p, mv), 0.0)
        mmem_ref[...] = (acc * mg).astype(mmem_ref.dtype)


def _proj_call(x, lw, *, decode, seq_len, pos0=0.0, conv_state=None, mk=None, mv=None):
    m, d = x.shape
    tm = _tile(seq_len if not decode else m, PROJ_ROWS)
    tps = 1 if decode else seq_len // tm
    nb = m // (tm * tps)
    grid = (m // tm,)
    row = lambda w: pl.BlockSpec((tm, w), lambda i: (i, 0))
    consts = [lw['pre_g'], lw['w_in'], lw['conv_w'], lw['q_norm_g'], lw['w_uq'], lw['kv_norm_g'],
              lw['w_ukT'], lw['inv_full']]
    in_specs = [row(d)] + [_const_spec(c.shape) for c in consts]
    qcat_shape = (nb, N_HEADS, tm * tps, QK_W)
    qcat_spec = pl.BlockSpec((1, N_HEADS, tm, QK_W), lambda i: (i // tps, 0, i % tps, 0))
    sds = jax.ShapeDtypeStruct
    common_out = [
        (sds((m, RWKV_SHIFT_W), F32), row(RWKV_SHIFT_W)),
        (sds((m, GROUP_W), F32), row(GROUP_W)),
        (sds((m, KV_RANK), F32), row(KV_RANK)),
        (sds((m, ROPE_DIM), F32), row(ROPE_DIM)),
        (sds((m, QK_W), BF16), row(QK_W)),
        (sds(qcat_shape, BF16), qcat_spec),
        (sds((m, GROUP_W), F32), row(GROUP_W)),
    ]
    if decode:
        ins = [x] + consts + [conv_state[:, 0], conv_state[:, 1]]
        in_specs += [row(GROUP_W), row(GROUP_W)]
        outs = ([(sds((m, GROUP_W), BF16), row(GROUP_W)),
                 (sds((m, GROUP_W), F32), row(GROUP_W)), (sds((m, GROUP_W), F32), row(GROUP_W))]
                + common_out
                + [(sds((m, GROUP_W), F32), row(GROUP_W)), (sds((m, GROUP_W), F32), row(GROUP_W))])
        scratch = []
    else:
        ins = [x] + consts + [mk, mv]
        n_mem = mk.shape[1]
        in_specs += [pl.BlockSpec((1, n_mem, GROUP_W), lambda i: (i // tps, 0, 0))] * 2
        outs = ([(sds((m, GROUP_W), BF16), row(GROUP_W)),
                 (sds((nb, CONV_K - 1, GROUP_W), F32),
                  pl.BlockSpec((1, CONV_K - 1, GROUP_W), lambda i: (i // tps, 0, 0)))]
                + common_out
                + [(sds((m, GROUP_W), BF16), row(GROUP_W))])
        scratch = [pltpu.VMEM((8, GROUP_W), F32), pltpu.VMEM((tm, LANES), F32), pltpu.VMEM((tm, LANES), F32)]
    return pl.pallas_call(
        functools.partial(_proj_body, decode, tps, pos0),
        grid=grid, in_specs=in_specs,
        out_specs=[o[1] for o in outs], out_shape=[o[0] for o in outs],
        scratch_shapes=scratch, compiler_params=_params(("arbitrary",)),
        name="proj_decode" if decode else "proj_prompt",
    )(*ins)


def _memkv_body(x_ref, g_ref, w_ref, k_ref, v_ref, kb_ref, vb_ref):
    kv = _mm(_rms(x_ref[...], g_ref[...]), w_ref[...])
    k, v = kv[:, :GROUP_W], kv[:, GROUP_W:]
    k_ref[...] = k
    v_ref[...] = v
    kb_ref[...] = k.astype(BF16)
    vb_ref[...] = v.astype(BF16)


def _memkv_call(mem, g, w_kv):
    m, d = mem.shape
    tm = _tile(m, PROJ_ROWS)
    row = lambda w: pl.BlockSpec((tm, w), lambda i: (i, 0))
    sds = jax.ShapeDtypeStruct
    return pl.pallas_call(
        _memkv_body, grid=(m // tm,),
        in_specs=[row(d), _const_spec(g.shape), _const_spec(w_kv.shape)],
        out_specs=[row(GROUP_W)] * 4,
        out_shape=[sds((m, GROUP_W), F32), sds((m, GROUP_W), F32), sds((m, GROUP_W), BF16), sds((m, GROUP_W), BF16)],
        compiler_params=_params(("arbitrary",)), name="mem_kv",
    )(mem, g, w_kv)


def _head_ones():
    r = lax.broadcasted_iota(jnp.int32, (GROUP_W, GROUP_W), 0) // HEAD_DIM
    c = lax.broadcasted_iota(jnp.int32, (GROUP_W, GROUP_W), 1) // HEAD_DIM
    return (r == c).astype(F32)


def _rwkv_tokens(mixed, prm, ones_bd):
    w0, a0, k_k, k_a, r_k = (prm[i:i + 1, :] for i in range(5))
    w2p, a2p = prm[8:8 + LANES, :], prm[8 + LANES:8 + 2 * LANES, :]
    r, k, v = mixed[:, 0:GROUP_W], mixed[:, GROUP_W:2 * GROUP_W], mixed[:, 2 * GROUP_W:3 * GROUP_W]
    wa = mixed[:, 3 * GROUP_W:RWKV_SHIFT_W]
    logw = -DECAY_SCALE * _sigmoid(w0 + _mm_hi(jnp.tanh(wa), w2p))
    a = _sigmoid(a0 + _mm_hi(wa, a2p))
    kkr = k * k_k
    kk = kkr * lax.rsqrt(_mm(kkr * kkr, ones_bd) + 1e-12)
    k2 = k * (1.0 + (a - 1.0) * k_a)
    bonus = _mm(r * k2 * r_k, ones_bd) * v
    return r, k2, v, logw, kk, a, bonus


def _rwkv_finish(y, bonus, gate, prm, ones_bd):
    lnx_g, lnx_b = prm[5:6, :], prm[6:7, :]
    mean = _mm(y, ones_bd) * (1.0 / HEAD_DIM)
    yc = y - mean
    var = _mm(yc * yc, ones_bd) * (1.0 / HEAD_DIM)
    return ((yc * lax.rsqrt(var + LNX_EPS)) * lnx_g + lnx_b + bonus) * gate


def _stack_heads(x, head_lane):
    return jnp.concatenate([jnp.where(head_lane == hh, x, 0.0) for hh in range(N_HEADS)], axis=0)


def _unstack_heads(x):
    c = x.shape[0] // N_HEADS
    return x[0:c] + x[c:2 * c] + x[2 * c:3 * c] + x[3 * c:4 * c]


def _rwkv_prompt_body(x_ref, gate_ref, mu_ref, prm_ref, o_ref, st_ref, prev_ref, s_ref):
    ct = x_ref.shape[1]
    si = pl.program_id(1)

    @pl.when(si == 0)
    def _():
        prev_ref[...] = jnp.zeros_like(prev_ref)
        s_ref[...] = jnp.zeros_like(s_ref)

    x = x_ref[0]
    row = lax.broadcasted_iota(jnp.int32, x.shape, 0)
    prev = jnp.where(row == 0, prev_ref[0:1, :], pltpu.roll(x, 1, 0))
    prev_ref[0:1, :] = x[ct - 1:ct, :]
    mixed = x + (prev - x) * mu_ref[...]
    prm = prm_ref[...]
    ones_bd = _head_ones()
    r, k2, v, logw, kk, a, bonus = _rwkv_tokens(mixed, prm, ones_bd)
    b = kk * a

    n = N_HEADS * CHUNK
    head_lane = lax.broadcasted_iota(jnp.int32, (CHUNK, GROUP_W), 1) // HEAD_DIM
    rr = lax.broadcasted_iota(jnp.int32, (n, n), 0)
    cc = lax.broadcasted_iota(jnp.int32, (n, n), 1)
    same_sub = (rr // SUB) == (cc // SUB)
    same_head = (rr // CHUNK) == (cc // CHUNK)
    eye = (rr == cc).astype(F32)
    t_w = lax.broadcasted_iota(jnp.int32, (CHUNK, GROUP_W), 0)
    s_w = lax.broadcasted_iota(jnp.int32, (CHUNK, GROUP_W), 1) % CHUNK
    strict_w, incl_w = t_w > s_w, t_w >= s_w
    eye_w = (t_w == s_w).astype(F32)
    eye_sub = (lax.broadcasted_iota(jnp.int32, (SUB, GROUP_W), 0)
               == lax.broadcasted_iota(jnp.int32, (SUB, GROUP_W), 1) % SUB).astype(F32)
    tri = (lax.broadcasted_iota(jnp.int32, (CHUNK, CHUNK), 0)
           >= lax.broadcasted_iota(jnp.int32, (CHUNK, CHUNK), 1)).astype(F32)

    def sub_square(z):
        return jnp.where(same_sub, jnp.concatenate([z] * (n // SUB), axis=0), 0.0)

    def sub_strip(z):
        out = z[0:SUB]
        for i in range(1, n // SUB):
            out = out + z[i * SUB:(i + 1) * SUB]
        return out

    chunks = range(ct // CHUNK)
    each = lambda f, *xs: [f(*(x[c] for x in xs)) for c in chunks]
    rows = lambda z: [z[c * CHUNK:(c + 1) * CHUNK] for c in chunks]
    stack = lambda z: _stack_heads(z, head_lane)
    lw = rows(logw)
    cw = each(lambda l: _mm_exact_lhs(tri, l), lw)
    cwl = each(lambda z: z[CHUNK - 1:CHUNK, :], cw)
    e_ng = each(lambda z: jnp.exp(-z), cw)
    e_w = each(lambda z, zl: jnp.exp(zl - z), cw, cwl)
    kkd = each(lambda x, z, l: x * jnp.exp(z - l), rows(kk), cw, lw)
    rd = each(lambda x, z: x * jnp.exp(z), rows(r), cw)
    kinvw = each(lambda x, e: x * e, rows(k2), e_w)
    binvw = each(lambda x, e: x * e, rows(b), e_w)
    v_w = rows(v)
    kkd_sq = each(stack, kkd)
    kinv_sq = each(lambda x, e: stack(x * e), rows(k2), e_ng)
    binv_sq = each(lambda x, e: stack(x * e), rows(b), e_ng)
    v_sq = each(stack, v_w)
    kr_rows = each(lambda x, y: jnp.concatenate([x, y], axis=0), kkd, rd)
    ab = each(_mm_nt, kr_rows, binv_sq)
    ak = each(_mm_nt, kr_rows, kinv_sq)
    a_bk = each(lambda z: jnp.where(strict_w, z[0:CHUNK], 0.0), ab)
    m_rb = each(lambda z: jnp.where(incl_w, z[CHUNK:2 * CHUNK], 0.0), ab)
    a_vk = each(lambda z: jnp.where(strict_w, z[0:CHUNK], 0.0), ak)
    m_rk = each(lambda z: jnp.where(incl_w, z[CHUNK:2 * CHUNK], 0.0), ak)
    a_sq = each(stack, a_bk)
    d1_sq = each(lambda x: jnp.where(same_sub, x, 0.0), a_sq)
    e1_sq = each(lambda x, y: x - y, a_sq, d1_sq)
    d1 = each(sub_strip, d1_sq)
    d2 = each(_mm, d1, d1_sq)
    t0 = each(lambda x: eye_sub - x, d1)
    r2 = each(lambda x, y, z: _mm(jnp.concatenate([x, y], axis=0), sub_square(z)), d2, t0, d2)
    d4 = each(lambda z: z[0:SUB], r2)
    t1 = each(lambda x, z: x + z[SUB:2 * SUB], t0, r2)
    r4 = each(lambda x, y, z: _mm(jnp.concatenate([x, y], axis=0), sub_square(z)), d4, t1, d4)
    d8 = each(lambda z: z[0:SUB], r4)
    t2 = each(lambda x, z: x + z[SUB:2 * SUB], t1, r4)
    dinv = each(lambda x, y: x + _mm(x, sub_square(y)), t2, d8)
    dinv_sq = each(sub_square, dinv)
    dinv_w = each(_unstack_heads, dinv_sq)
    nn = each(_mm, dinv_w, e1_sq)
    nn_sq = each(stack, nn)
    n2 = each(_mm, nn, nn_sq)
    t3 = each(lambda x, y: _mm(eye_w - x, eye + stack(y)), nn, n2)
    tinv = each(_mm, t3, dinv_sq)
    xv = each(lambda x, y, z: _mm(jnp.concatenate([x, y], axis=0), z), a_vk, m_rk, v_sq)
    x_w = each(lambda z: z[0:CHUNK], xv)
    kkdp = each(_mm, tinv, kkd_sq)
    vp = each(_mm, tinv, each(stack, x_w))
    kkdp_sq = each(stack, kkdp)
    rq = each(lambda x, y, z: x - _mm(y, z), rd, m_rb, kkdp_sq)
    y_in = each(lambda z, a3, a4: z[CHUNK:2 * CHUNK] - _mm(a3, stack(a4)), xv, m_rb, vp)
    phi = each(lambda zl, x, y: eye * jnp.exp(zl) - jnp.where(same_head, _mm_tn(x, y), 0.0), cwl, kkdp, binvw)
    g = each(lambda a1, a2, a3, a4: _unstack_heads(jnp.where(
        same_head, _mm_tn(jnp.concatenate([a1, -a3], axis=0), jnp.concatenate([a2, a4], axis=0)), 0.0)),
        v_w, kinvw, vp, binvw)
    state = s_ref[...]
    ys = []
    for c in chunks:
        ys.append(_mm_nt(rq[c], stack(state)) + y_in[c])
        state = _mm(state, phi[c]) + g[c]
    s_ref[...] = state
    st_ref[0] = state
    y = jnp.concatenate(ys, axis=0) if len(ys) > 1 else ys[0]
    o_ref[...] = _rwkv_finish(y, bonus, gate_ref[...], prm, ones_bd).astype(o_ref.dtype)


def _rwkv_prompt_call(r_sh, gate, lw, nb, seq_len):
    ct = _tile(seq_len, RWKV_ROWS)
    ns = seq_len // ct
    x3 = r_sh.reshape(nb, seq_len, RWKV_SHIFT_W)
    sds = jax.ShapeDtypeStruct
    out, st = pl.pallas_call(
        _rwkv_prompt_body, grid=(nb, ns),
        in_specs=[pl.BlockSpec((1, ct, RWKV_SHIFT_W), lambda b, s: (b, s, 0)),
                  pl.BlockSpec((ct, GROUP_W), lambda b, s: (b * ns + s, 0)),
                  _const_spec(lw['mu'].shape), _const_spec(lw['rwkv_prm'].shape)],
        out_specs=[pl.BlockSpec((ct, GROUP_W), lambda b, s: (b * ns + s, 0)),
                   pl.BlockSpec((1, HEAD_DIM, GROUP_W), lambda b, s: (b, 0, 0))],
        out_shape=[sds((nb * seq_len, GROUP_W), BF16), sds((nb, HEAD_DIM, GROUP_W), F32)],
        scratch_shapes=[pltpu.VMEM((8, RWKV_SHIFT_W), F32),
                        pltpu.VMEM((HEAD_DIM, GROUP_W), F32)],
        compiler_params=_params(("arbitrary", "arbitrary")), name="rwkv_prompt",
    )(x3, gate, lw['mu'], lw['rwkv_prm'])
    st = st.reshape(nb, HEAD_DIM, N_HEADS, HEAD_DIM).transpose(0, 2, 1, 3)
    return out, st


def _rwkv_step_body(x_ref, prev_ref, gate_ref, mu_ref, prm_ref, s_ref, o_ref, so_ref, yt_ref, vt_ref):
    hh = pl.program_id(0)
    x = x_ref[...]
    mixed = x + (prev_ref[...] - x) * mu_ref[...]
    prm = prm_ref[...]
    ones_bd = _head_ones()
    r, k2, v, logw, kk, a, bonus = _rwkv_tokens(mixed, prm, ones_bd)
    row0 = pl.multiple_of(hh * HEAD_DIM, HEAD_DIM)
    for i, z in enumerate((kk, jnp.exp(logw), kk * a, v, k2, r)):
        vt_ref[i] = z.T
    head = lambda i: vt_ref[i, pl.ds(row0, HEAD_DIM), :]
    st = s_ref[...]
    sa = jnp.sum(st * head(0)[None, :, :], axis=1)
    st = st * head(1)[None, :, :] - sa[:, None, :] * head(2)[None, :, :] + head(3)[:, None, :] * head(4)[None, :, :]
    so_ref[...] = st
    yt_ref[pl.ds(row0, HEAD_DIM), :] = jnp.sum(st * head(5)[None, :, :], axis=1)

    @pl.when(hh == N_HEADS - 1)
    def _():
        o_ref[...] = _rwkv_finish(yt_ref[...].T, bonus, gate_ref[...], prm, ones_bd).astype(o_ref.dtype)


def _rwkv_step_call(layer, r_sh, prev, gate, state_t, lw):
    m = r_sh.shape[0]
    full = lambda w: pl.BlockSpec((m, w), lambda h: (0, 0))
    sds = jax.ShapeDtypeStruct
    return pl.pallas_call(
        _rwkv_step_body, grid=(N_HEADS,),
        in_specs=[full(RWKV_SHIFT_W), full(RWKV_SHIFT_W), full(GROUP_W),
                  _const_spec(lw['mu'].shape), _const_spec(lw['rwkv_prm'].shape),
                  pl.BlockSpec((None, None, HEAD_DIM, HEAD_DIM, m), lambda h: (layer, h, 0, 0, 0))],
        out_specs=[full(GROUP_W), pl.BlockSpec((None, HEAD_DIM, HEAD_DIM, m), lambda h: (h, 0, 0, 0))],
        out_shape=[sds((m, GROUP_W), BF16), sds((N_HEADS, HEAD_DIM, HEAD_DIM, m), F32)],
        scratch_shapes=[pltpu.VMEM((GROUP_W, m), F32), pltpu.VMEM((6, GROUP_W, m), F32)],
        compiler_params=_params(("arbitrary",)), name="rwkv_step",
    )(r_sh, prev, gate, lw['mu'], lw['rwkv_prm'], state_t)


def _softmax_update(s, v, m_sc, l_sc, acc_sc):
    m_prev = m_sc[...]
    m_new = jnp.maximum(m_prev, jnp.max(s, axis=-1, keepdims=True))
    alpha = jnp.exp(m_prev - m_new)
    p = jnp.exp(s - m_new)
    l_sc[...] = alpha * l_sc[...] + jnp.sum(p, axis=-1, keepdims=True)
    acc_sc[...] = alpha * acc_sc[...] + _mm(p, v)
    m_sc[...] = m_new


def _flash_body(q_ref, k_ref, gate_ref, wuv_ref, x_ref, mconv_ref, mrwkv_ref, mmem_ref, wout_ref, pg_ref,
                o_ref, sa_ref, sb_ref, m_ref, acc_ref):
    qi = pl.program_id(1)
    tq = q_ref.shape[2]

    def keys(kb):
        return k_ref[0, pl.ds(pl.multiple_of(kb * tq, tq), tq), :]

    def scores(kb, s_ref):
        k = keys(kb)
        for hh in range(N_HEADS):
            s_ref[hh] = _mm_nt(q_ref[0, hh], k)

    def update(s_ref, kb, masked):
        k = keys(kb)
        for hh in range(N_HEADS):
            s_h = s_ref[hh]
            if masked:
                s_h = jnp.where(lax.broadcasted_iota(jnp.int32, s_h.shape, 1)
                                <= lax.broadcasted_iota(jnp.int32, s_h.shape, 0), s_h, NEG_INF)
            m_prev = m_ref[hh]
            m_new = jnp.maximum(m_prev, jnp.max(s_h, axis=-1, keepdims=True))
            p = jnp.exp(s_h - jnp.concatenate([m_new] * (tq // LANES), axis=1))
            alpha = jnp.exp(m_prev - m_new)
            acc_ref[hh] = jnp.concatenate([alpha] * (QK_W // LANES), axis=1) * acc_ref[hh] + _mm(p, k)
            m_ref[hh] = m_new

    def finish(s_ref):
        update(s_ref, qi, True)
        mla = jnp.zeros((tq, GROUP_W), F32)
        for hh in range(N_HEADS):
            acc = acc_ref[hh]
            o = acc[:, 0:KV_RANK] / acc[:, ONES_LANE:ONES_LANE + 1]
            mla = mla + _mm(o, wuv_ref[hh])
        mixed = (mconv_ref[...], mrwkv_ref[...], (mla * gate_ref[...]).astype(BF16), mmem_ref[...])
        y = jnp.zeros(x_ref.shape, F32)
        for i, m_i in enumerate(mixed):
            y = y + _mm(m_i, wout_ref[i * GROUP_W:(i + 1) * GROUP_W, :])
        o_ref[...] = x_ref[...] + _rms(y, pg_ref[...])

    m_ref[...] = jnp.full_like(m_ref, NEG_INF)
    acc_ref[...] = jnp.zeros_like(acc_ref)
    scores(0, sa_ref)

    def pair(j, _):
        kb = 2 * j
        scores(kb + 1, sb_ref)
        update(sa_ref, kb, False)
        scores(kb + 2, sa_ref)
        update(sb_ref, kb + 1, False)
        return 0

    lax.fori_loop(0, qi // 2, pair, 0)

    @pl.when(qi % 2 == 1)
    def _():
        scores(qi, sb_ref)
        update(sa_ref, qi - 1, False)
        finish(sb_ref)

    @pl.when(qi % 2 == 0)
    def _():
        finish(sa_ref)


def _flash_call(qcat, kcat, gate, w_uv_pad, x, m_conv, m_rwkv, m_mem, w_out, post_g, seq_len):
    nb = qcat.shape[0]
    d = x.shape[1]
    tq = _tile(seq_len, FLASH_ROWS)
    nq = seq_len // tq
    k3 = kcat.reshape(nb, seq_len, QK_W)
    row = lambda w: pl.BlockSpec((tq, w), lambda b, i: (b * nq + i, 0))
    return pl.pallas_call(
        _flash_body, grid=(nb, nq),
        in_specs=[pl.BlockSpec((1, N_HEADS, tq, QK_W), lambda b, i: (b, 0, i, 0)),
                  pl.BlockSpec((1, seq_len, QK_W), lambda b, i: (b, 0, 0)),
                  row(GROUP_W), _const_spec(w_uv_pad.shape),
                  row(d), row(GROUP_W), row(GROUP_W), row(GROUP_W),
                  _const_spec(w_out.shape), _const_spec(post_g.shape)],
        out_specs=row(d),
        out_shape=jax.ShapeDtypeStruct((nb * seq_len, d), F32),
        scratch_shapes=[pltpu.VMEM((N_HEADS, tq, tq), F32), pltpu.VMEM((N_HEADS, tq, tq), F32),
                        pltpu.VMEM((N_HEADS, tq, LANES), F32), pltpu.VMEM((N_HEADS, tq, QK_W), F32)],
        compiler_params=_params(("arbitrary", "arbitrary")), name="flash_mla",
    )(qcat, k3, gate, w_uv_pad, x, m_conv, m_rwkv, m_mem, w_out, post_g)


def _decode_body(layer, nch, pg, pt_ref, q_ref, cnew_ref, knew_ref, gate_ref, wuv_ref, ckv_hbm, kr_hbm,
                 o_ref, cbuf, kbuf, sem, m_sc, l_sc, acc_sc):
    b, c = pl.program_id(0), pl.program_id(1)
    step = b * nch + c
    total = pl.num_programs(0) * nch
    slot = step % 2
    page_len = cbuf.shape[2]

    def page_copies(st, sl):
        bb, cc = st // nch, st % nch
        out = []
        for p in range(pg):
            page = pt_ref[bb, cc * pg + p]
            out.append(pltpu.make_async_copy(ckv_hbm.at[layer, page], cbuf.at[sl, p], sem.at[0, sl]))
            out.append(pltpu.make_async_copy(kr_hbm.at[layer, page], kbuf.at[sl, :, pl.ds(p * page_len, page_len)],
                                             sem.at[1, sl]))
        return out

    @pl.when(step == 0)
    def _():
        for cp in page_copies(0, 0):
            cp.start()

    @pl.when(step + 1 < total)
    def _():
        for cp in page_copies(step + 1, 1 - slot):
            cp.start()

    @pl.when(c == 0)
    def _():
        m_sc[...] = jnp.full_like(m_sc, NEG_INF)
        l_sc[...] = jnp.zeros_like(l_sc)
        acc_sc[...] = jnp.zeros_like(acc_sc)

    for cp in page_copies(step, slot):
        cp.wait()

    ck = cbuf[slot].reshape(pg * page_len, KV_RANK).astype(BF16)
    q = q_ref[0]
    s = _mm_nt(q[:, 0:KV_RANK], ck) + _mm(q[:, KV_RANK:QK_DIM], kbuf[slot])
    _softmax_update(s, ck, m_sc, l_sc, acc_sc)

    @pl.when(c == nch - 1)
    def _():
        qf = q.astype(F32)
        cn, kn = cnew_ref[0], knew_ref[0]
        s_new = (jnp.sum(qf[:, 0:KV_RANK] * cn, axis=-1, keepdims=True)
                 + jnp.sum(qf[:, KV_RANK:QK_DIM] * kn, axis=-1, keepdims=True))
        m_prev = m_sc[...]
        m_new = jnp.maximum(m_prev, s_new)
        alpha, p_new = jnp.exp(m_prev - m_new), jnp.exp(s_new - m_new)
        l_fin = alpha * l_sc[...] + p_new
        o = (alpha * acc_sc[...] + p_new * cn) / l_fin
        out = jnp.zeros((1, GROUP_W), F32)
        for hh in range(N_HEADS):
            out = out + _mm(o, wuv_ref[hh])[hh:hh + 1, :]
        o_ref[0] = out * gate_ref[0]


def _decode_call(layer, page_table, q16, ckv_new, kr_new, gate, w_uv_pad, cache_ckv, cache_krope):
    nb, n_pages = page_table.shape
    page = cache_ckv.shape[2]
    pg = _tile(n_pages, DECODE_PAGES)
    nch = n_pages // pg
    blk = lambda shape: pl.BlockSpec((1,) + shape, lambda b, c, pt: (b, 0, 0))
    grid_spec = pltpu.PrefetchScalarGridSpec(
        num_scalar_prefetch=1, grid=(nb, nch),
        in_specs=[blk((16, QK_W)), blk((1, KV_RANK)), blk((1, ROPE_DIM)), blk((1, GROUP_W)),
                  pl.BlockSpec(w_uv_pad.shape, lambda b, c, pt: (0, 0, 0)),
                  pl.BlockSpec(memory_space=pl.ANY), pl.BlockSpec(memory_space=pl.ANY)],
        out_specs=blk((1, GROUP_W)),
        scratch_shapes=[pltpu.VMEM((2, pg, page, KV_RANK), F32), pltpu.VMEM((2, ROPE_DIM, pg * page), F32),
                        pltpu.SemaphoreType.DMA((2, 2)),
                        pltpu.VMEM((16, 1), F32), pltpu.VMEM((16, 1), F32), pltpu.VMEM((16, KV_RANK), F32)])
    out = pl.pallas_call(
        functools.partial(_decode_body, layer, nch, pg), grid_spec=grid_spec,
        out_shape=jax.ShapeDtypeStruct((nb, 1, GROUP_W), F32),
        compiler_params=_params(("arbitrary", "arbitrary")), name="paged_decode",
    )(page_table, q16, ckv_new[:, None, :], kr_new[:, None, :], gate[:, None, :], w_uv_pad,
      cache_ckv, cache_krope)
    return out.reshape(nb, GROUP_W)


def _memattn_body(q_ref, gate_ref, k_ref, v_ref, o_ref):
    tb = q_ref.shape[0]
    head = lax.broadcasted_iota(jnp.int32, (8, GROUP_W), 1) // HEAD_DIM
    sel = head == lax.broadcasted_iota(jnp.int32, (8, GROUP_W), 0)
    for bi in range(tb):
        q_bd = jnp.where(sel, q_ref[bi], 0.0)
        s = _mm(q_bd, k_ref[0, bi])
        p = jnp.exp(s - jnp.max(s, axis=-1, keepdims=True))
        p = p / jnp.sum(p, axis=-1, keepdims=True)
        o = jnp.where(sel, _mm_nt(p, v_ref[0, bi]), 0.0)
        o_ref[bi] = jnp.sum(o, axis=0, keepdims=True) * gate_ref[bi]


def _memattn_call(layer, mem_q, gate, mem_k, mem_v):
    m = mem_q.shape[0]
    n_mem = mem_k.shape[3]
    tb = _tile(m, MEMATTN_TOKENS)
    vec = pl.BlockSpec((tb, 1, GROUP_W), lambda i: (i, 0, 0))
    kv = pl.BlockSpec((1, tb, GROUP_W, n_mem), lambda i: (layer, i, 0, 0))
    out = pl.pallas_call(
        _memattn_body, grid=(m // tb,),
        in_specs=[vec, vec, kv, kv], out_specs=vec,
        out_shape=jax.ShapeDtypeStruct((m, 1, GROUP_W), F32),
        compiler_params=_params(("arbitrary",)), name="mem_attn_decode",
    )(mem_q[:, None, :], gate[:, None, :], mem_k, mem_v)
    return out.reshape(m, GROUP_W)


def _outproj_body(x_ref, m0_ref, m1_ref, m2_ref, m3_ref, w_ref, g_ref, o_ref):
    acc = jnp.zeros(x_ref.shape, F32)
    for i, m_ref in enumerate((m0_ref, m1_ref, m2_ref, m3_ref)):
        acc = acc + _mm(m_ref[...], w_ref[i * GROUP_W:(i + 1) * GROUP_W, :])
    o_ref[...] = x_ref[...] + _rms(acc, g_ref[...])


def _outproj_call(x, mixed, w_out, g):
    m, d = x.shape
    tm = _tile(m, PROJ_ROWS)
    row = lambda w: pl.BlockSpec((tm, w), lambda i: (i, 0))
    return pl.pallas_call(
        _outproj_body, grid=(m // tm,),
        in_specs=[row(d)] + [row(GROUP_W)] * 4 + [_const_spec(w_out.shape), _const_spec(g.shape)],
        out_specs=row(d), out_shape=jax.ShapeDtypeStruct((m, d), F32),
        compiler_params=_params(("arbitrary",)), name="out_proj",
    )(x, *mixed, w_out, g)


def _prep_layer(l, pre_norm_g, post_norm_g, w_in, w_out, conv_w, rwkv_mu, rwkv_w0, rwkv_w2, rwkv_a0, rwkv_a2,
                rwkv_k_k, rwkv_k_a, rwkv_r_k, rwkv_lnx_g, rwkv_lnx_b, mla_q_norm_g, mla_w_uq, mla_kv_norm_g,
                mla_w_uk, mla_w_uv, mem_norm_g, w_mem_k, w_mem_v):
    d = w_in.shape[1]
    kr0 = _C_MLAG
    w = w_in[l]
    w_in_p = jnp.concatenate([w[:, :kr0], w[:, kr0 + ROPE_DIM:], w[:, kr0:kr0 + ROPE_DIM],
                              jnp.zeros((d, LANES - ROPE_DIM), F32)], axis=1).astype(BF16)
    uq = mla_w_uq[l].reshape(Q_RANK, N_HEADS, NOPE_DIM + ROPE_DIM)
    w_uq = jnp.concatenate([uq[:, :, :NOPE_DIM].reshape(Q_RANK, -1), uq[:, :, NOPE_DIM:].reshape(Q_RANK, -1)],
                           axis=1).astype(BF16)
    w_uv_pad = jnp.zeros((N_HEADS, KV_RANK, GROUP_W), F32)
    for hh in range(N_HEADS):
        w_uv_pad = w_uv_pad.at[hh, :, hh * HEAD_DIM:(hh + 1) * HEAD_DIM].set(mla_w_uv[l][:, hh, :])
    w_uk_bd = jnp.zeros((N_HEADS * NOPE_DIM, N_HEADS * KV_RANK), F32)
    for hh in range(N_HEADS):
        w_uk_bd = w_uk_bd.at[hh * NOPE_DIM:(hh + 1) * NOPE_DIM, hh * KV_RANK:(hh + 1) * KV_RANK].set(
            mla_w_uk[l][:, hh, :].T)
    half = ROPE_DIM // 2
    inv = jnp.power(ROPE_BASE, -jnp.arange(half, dtype=F32) / half)
    zl = jnp.zeros((LORA_W, GROUP_W), F32)
    vecs = jnp.stack([rwkv_w0[l], rwkv_a0[l], rwkv_k_k[l], rwkv_k_a[l], rwkv_r_k[l].reshape(-1),
                      rwkv_lnx_g[l], rwkv_lnx_b[l], jnp.zeros((GROUP_W,), F32)])
    return {
        'pre_g': pre_norm_g[l][None], 'post_g': post_norm_g[l][None], 'w_in': w_in_p,
        'w_out': w_out[l].astype(BF16), 'conv_w': conv_w[l], 'q_norm_g': mla_q_norm_g[l][None], 'w_uq': w_uq,
        'kv_norm_g': mla_kv_norm_g[l][None], 'w_ukT': w_uk_bd.astype(BF16),
        'w_uv_pad': w_uv_pad.astype(BF16), 'inv_full': jnp.tile(inv, LANES // half)[None],
        'mu': rwkv_mu[l][None],
        'rwkv_prm': jnp.concatenate([vecs, rwkv_w2[l], zl, zl, rwkv_a2[l]], axis=0),
        'mem_g': mem_norm_g[l][None],
        'w_mem_kv': jnp.concatenate([w_mem_k[l], w_mem_v[l]], axis=1).astype(BF16),
    }


def kernel(x_prompt, x_sample, cache_ckv, cache_krope, cache_mem_k, cache_mem_v, state_conv, state_rwkv_shift, state_rwkv, page_table, mem_prompt, pre_norm_g, post_norm_g, w_in, w_out, conv_w, rwkv_mu, rwkv_w0, rwkv_w2, rwkv_a0, rwkv_a2, rwkv_k_k, rwkv_k_a, rwkv_r_k, rwkv_lnx_g, rwkv_lnx_b, mla_q_norm_g, mla_w_uq, mla_kv_norm_g, mla_w_uk, mla_w_uv, mem_norm_g, w_mem_k, w_mem_v):
    bp, sp, d = x_prompt.shape
    bs, ts, _ = x_sample.shape
    assert ts == 1
    depth = w_in.shape[0]
    n_mem = mem_prompt.shape[1]
    past_len = page_table.shape[1] * cache_ckv.shape[2]
    mem_k4 = jnp.transpose(cache_mem_k, (0, 1, 3, 4, 2)).reshape(depth, bs, GROUP_W, n_mem)
    mem_v4 = jnp.transpose(cache_mem_v, (0, 1, 3, 4, 2)).reshape(depth, bs, GROUP_W, n_mem)
    cache_krope_t = jnp.swapaxes(cache_krope, 2, 3)
    state_rwkv_t = jnp.transpose(state_rwkv, (0, 2, 3, 4, 1))
    xp = x_prompt.reshape(bp * sp, d)
    xs = x_sample.reshape(bs, d)
    mem2 = mem_prompt.reshape(bp * n_mem, d)
    outs = [[] for _ in range(12)]
    for l in range(depth):
        lw = _prep_layer(l, pre_norm_g, post_norm_g, w_in, w_out, conv_w, rwkv_mu, rwkv_w0, rwkv_w2, rwkv_a0,
                         rwkv_a2, rwkv_k_k, rwkv_k_a, rwkv_r_k, rwkv_lnx_g, rwkv_lnx_b, mla_q_norm_g, mla_w_uq,
                         mla_kv_norm_g, mla_w_uk, mla_w_uv, mem_norm_g, w_mem_k, w_mem_v)
        mk, mv, mk_b, mv_b = _memkv_call(mem2, lw['mem_g'], lw['w_mem_kv'])
        (m_conv, conv_p, r_sh, r_gate, ckv, kr, kcat, qcat, mla_gate, m_mem) = _proj_call(
            xp, lw, decode=False, seq_len=sp,
            mk=mk_b.reshape(bp, n_mem, GROUP_W), mv=mv_b.reshape(bp, n_mem, GROUP_W))
        m_rwkv, st_p = _rwkv_prompt_call(r_sh, r_gate, lw, bp, sp)
        xp = _flash_call(qcat, kcat, mla_gate, lw['w_uv_pad'], xp, m_conv, m_rwkv, m_mem, lw['w_out'],
                         lw['post_g'], sp)
        sh_p = r_sh.reshape(bp, sp, RWKV_SHIFT_W)[:, -1]
        (s_conv, cn0, cn1, r_sh_s, r_gate_s, ckv_s, kr_s, _, qcat_s, mla_gate_s, mem_q_s, mem_gate_s) = _proj_call(
            xs, lw, decode=True, seq_len=1, pos0=float(past_len), conv_state=state_conv[l])
        s_rwkv, st_s = _rwkv_step_call(l, r_sh_s, state_rwkv_shift[l], r_gate_s, state_rwkv_t, lw)
        q16 = jnp.pad(jnp.transpose(qcat_s[0], (1, 0, 2)), ((0, 0), (0, 16 - N_HEADS), (0, 0)))
        s_mla = _decode_call(l, page_table, q16, ckv_s, kr_s, mla_gate_s, lw['w_uv_pad'], cache_ckv, cache_krope_t)
        s_mem = _memattn_call(l, mem_q_s, mem_gate_s, mem_k4, mem_v4)
        xs = _outproj_call(xs, (s_conv, s_rwkv, s_mla, s_mem), lw['w_out'], lw['post_g'])
        vals = (ckv.reshape(bp, sp, KV_RANK), ckv_s.reshape(bs, ts, KV_RANK),
                kr.reshape(bp, sp, ROPE_DIM), kr_s.reshape(bs, ts, ROPE_DIM),
                mk.reshape(bp, n_mem, N_HEADS, HEAD_DIM), mv.reshape(bp, n_mem, N_HEADS, HEAD_DIM),
                conv_p, jnp.stack([cn0, cn1], axis=1), sh_p, r_sh_s, st_p, st_s)
        for o, v in zip(outs, vals):
            o.append(v)
    res = [jnp.stack(o) for o in outs]
    res[-1] = jnp.transpose(res[-1], (0, 4, 1, 2, 3))
    return (xp.reshape(bp, sp, d), xs.reshape(bs, ts, d)) + tuple(res)
```

```python
import functools

import jax
import jax.numpy as jnp
import numpy as np
from jax import lax
from jax.experimental import pallas as pl
from jax.experimental.pallas import tpu as pltpu

F32 = jnp.float32
BF16 = jnp.bfloat16

HEAD_DIM = 64
N_HEADS = 4
GROUP_W = N_HEADS * HEAD_DIM
CONV_K = 3
LORA_W = 64
RWKV_SHIFT_W = 3 * GROUP_W + 2 * LORA_W
Q_RANK = 256
KV_RANK = 128
NOPE_DIM = 64
ROPE_DIM = 32
QK_DIM = KV_RANK + ROPE_DIM
QK_W = 2 * KV_RANK
ONES_LANE = QK_DIM
RMS_EPS = 1e-6
LNX_EPS = 64e-5
ROPE_BASE = 10000.0
DECAY_SCALE = 0.6065306597
MLA_SCALE = (NOPE_DIM + ROPE_DIM) ** -0.5
MEM_SCALE = HEAD_DIM ** -0.5
NEG_INF = -1e30

LANES = 128
CHUNK = 64
SUB = 16
VMEM_LIMIT = 56 * 1024 * 1024
PROJ_ROWS = 512
RWKV_ROWS = 512
FLASH_ROWS = 512
DECODE_PAGES = 128
MEMATTN_TOKENS = 8

_C_CONV, _C_RSH, _C_RGATE, _C_QDOWN, _C_CKV, _C_MLAG, _C_MEMQ, _C_MEMG, _C_KR, _C_END = (
    0, 1024, 1920, 2176, 2432, 2560, 2816, 3072, 3328, 3456)


def _mm(a, b):
    return jnp.dot(a.astype(BF16), b.astype(BF16), preferred_element_type=F32)


def _mm_nt(a, b):
    return lax.dot_general(a.astype(BF16), b.astype(BF16), (((1,), (1,)), ((), ())),
                           preferred_element_type=F32)


def _mm_tn(a, b):
    return lax.dot_general(a.astype(BF16), b.astype(BF16), (((0,), (0,)), ((), ())),
                           preferred_element_type=F32)


def _split_bf16(x, n):
    parts = []
    for _ in range(n):
        p = x.astype(BF16)
        parts.append(p)
        x = x - p.astype(F32)
    return parts


def _mm_hi(a, b):
    (a1, a2), (b1, b2) = _split_bf16(a, 2), _split_bf16(b, 2)
    dot = lambda x, y: jnp.dot(x, y, preferred_element_type=F32)
    return dot(a1, b1) + (dot(a1, b2) + dot(a2, b1))


def _mm_exact_lhs(a01, b):
    a = a01.astype(BF16)
    b1, b2, b3 = _split_bf16(b, 3)
    dot = lambda x: jnp.dot(a, x, preferred_element_type=F32)
    return dot(b1) + (dot(b2) + dot(b3))


def _sigmoid(x):
    return 1.0 / (1.0 + jnp.exp(-x))


def _silu(x):
    return x * _sigmoid(x)


def _rms(x, g):
    return x * lax.rsqrt(jnp.mean(x * x, axis=-1, keepdims=True) + RMS_EPS) * g


def _params(sem):
    return pltpu.CompilerParams(dimension_semantics=sem, vmem_limit_bytes=VMEM_LIMIT)


def _const_spec(shape):
    nd = len(shape)
    return pl.BlockSpec(shape, lambda *_: (0,) * nd)


def _tile(n, pref):
    t = min(n, pref)
    assert n % t == 0, (n, pref)
    return t


def _proj_body(decode, tiles_per_seq, pos0, *refs):
    if decode:
        (x_ref, g_ref, w_ref, cw_ref, qg_ref, wuq_ref, kvg_ref, wuk_ref, inv_ref, cb0_ref, cb1_ref,
         mconv_ref, cn0_ref, cn1_ref, rsh_ref, rgate_ref, ckv_ref, kr_ref, kcat_ref, qcat_ref,
         mlag_ref, memq_ref, memg_ref) = refs
    else:
        (x_ref, g_ref, w_ref, cw_ref, qg_ref, wuq_ref, kvg_ref, wuk_ref, inv_ref, mk_ref, mv_ref,
         mconv_ref, cst_ref, rsh_ref, rgate_ref, ckv_ref, kr_ref, kcat_ref, qcat_ref,
         mlag_ref, mmem_ref, carry_ref, cos_row_ref, sin_row_ref) = refs
    tm = x_ref.shape[0]
    t = pl.program_id(0) % tiles_per_seq
    h = _rms(x_ref[...], g_ref[...]).astype(BF16)

    def proj(lo, hi):
        return jnp.dot(h, w_ref[:, lo:hi], preferred_element_type=F32)

    cv = proj(_C_CONV, _C_RSH)
    c_b, c_c, c_x, c_g = (cv[:, i * GROUP_W:(i + 1) * GROUP_W] for i in range(4))
    u = c_c * c_x
    w0, w1, w2 = cw_ref[0:1, :], cw_ref[1:2, :], cw_ref[2:3, :]
    if decode:
        b0, b1 = cb0_ref[...], cb1_ref[...]
        y = b0 * w0 + b1 * w1 + u * w2
        cn0_ref[...] = b1
        cn1_ref[...] = u
    else:
        @pl.when(t == 0)
        def _():
            carry_ref[...] = jnp.zeros_like(carry_ref)
        p0, p1 = carry_ref[0:1, :], carry_ref[1:2, :]
        row = lax.broadcasted_iota(jnp.int32, u.shape, 0)
        u1 = jnp.where(row == 0, p1, pltpu.roll(u, 1, 0))
        u2 = jnp.where(row == 0, p0, jnp.where(row == 1, p1, pltpu.roll(u, 2, 0)))
        y = u2 * w0 + u1 * w1 + u * w2
        carry_ref[0:2, :] = u[tm - 2:tm, :]
        cst_ref[0] = u[tm - 2:tm, :]
    mconv_ref[...] = (c_b * y * _silu(c_g)).astype(mconv_ref.dtype)

    rsh_ref[...] = proj(_C_RSH, _C_RGATE)
    rgate_ref[...] = _silu(proj(_C_RGATE, _C_QDOWN))

    if decode:
        ang = jnp.full((tm, 1), pos0, F32) * inv_ref[...]
        cos, sin = jnp.cos(ang), jnp.sin(ang)
    else:
        @pl.when(pl.program_id(0) == 0)
        def _():
            ang_row = lax.broadcasted_iota(jnp.int32, (tm, 1), 0).astype(F32) * inv_ref[...]
            cos_row_ref[...] = jnp.cos(ang_row)
            sin_row_ref[...] = jnp.sin(ang_row)
        ang_tile = (t * tm).astype(F32) * inv_ref[...]
        cos_t, sin_t = jnp.cos(ang_tile), jnp.sin(ang_tile)
        cos_r, sin_r = cos_row_ref[...], sin_row_ref[...]
        cos = cos_t * cos_r - sin_t * sin_r
        sin = sin_t * cos_r + cos_t * sin_r
    first = (lax.broadcasted_iota(jnp.int32, (tm, LANES), 1) % ROPE_DIM) < (ROPE_DIM // 2)

    def rope(v):
        partner = jnp.where(first, -pltpu.roll(v, LANES - ROPE_DIM // 2, 1), pltpu.roll(v, ROPE_DIM // 2, 1))
        return v * cos + partner * sin

    qn = _rms(proj(_C_QDOWN, _C_CKV), qg_ref[...])
    q = _mm(qn, wuq_ref[...])
    q_rope = rope(q[:, GROUP_W:GROUP_W + LANES]) * MLA_SCALE
    lane = lax.broadcasted_iota(jnp.int32, (tm, LANES), 1)
    q_lat = _mm(q[:, 0:GROUP_W], wuk_ref[...]) * MLA_SCALE
    for hh in range(N_HEADS):
        qcat_ref[0, hh, :, 0:KV_RANK] = q_lat[:, hh * KV_RANK:(hh + 1) * KV_RANK].astype(qcat_ref.dtype)
        q_r = q_rope if hh == 0 else pltpu.roll(q_rope, LANES - hh * ROPE_DIM, 1)
        qcat_ref[0, hh, :, KV_RANK:QK_W] = jnp.where(lane < ROPE_DIM, q_r, 0.0).astype(qcat_ref.dtype)
    ckv = _rms(proj(_C_CKV, _C_MLAG), kvg_ref[...])
    krope = rope(proj(_C_KR, _C_END))
    ckv_ref[...] = ckv
    kr_ref[...] = krope[:, 0:ROPE_DIM]
    kcat_ref[:, 0:KV_RANK] = ckv.astype(kcat_ref.dtype)
    kcat_ref[:, KV_RANK:QK_W] = jnp.where(lane == ONES_LANE - KV_RANK, 1.0, krope).astype(kcat_ref.dtype)
    mlag_ref[...] = _silu(proj(_C_MLAG, _C_MEMQ))

    mq = proj(_C_MEMQ, _C_MEMG) * MEM_SCALE
    mg = _silu(proj(_C_MEMG, _C_KR))
    if decode:
        memq_ref[...] = mq
        memg_ref[...] = mg
    else:
        mk, mv = mk_ref[0], mv_ref[0]
        head = lax.broadcasted_iota(jnp.int32, (tm, GROUP_W), 1) // HEAD_DIM
        acc = jnp.zeros((tm, GROUP_W), F32)
        for hh in range(N_HEADS):
            s = _mm_nt(jnp.where(head == hh, mq, 0.0), mk)
            p = jnp.exp(s - jnp.max(s, axis=-1, keepdims=True))
            p = p / jnp.sum(p, axis=-1, keepdims=True)
            acc = acc + jnp.where(head == hh, _mm(p, mv), 0.0)
        mmem_ref[...] = (acc * mg).astype(mmem_ref.dtype)


def _proj_call(x, lw, *, decode, seq_len, pos0=0.0, conv_state=None, mk=None, mv=None):
    m, d = x.shape
    tm = _tile(seq_len if not decode else m, PROJ_ROWS)
    tps = 1 if decode else seq_len // tm
    nb = m // (tm * tps)
    grid = (m // tm,)
    row = lambda w: pl.BlockSpec((tm, w), lambda i: (i, 0))
    consts = [lw['pre_g'], lw['w_in'], lw['conv_w'], lw['q_norm_g'], lw['w_uq'], lw['kv_norm_g'],
              lw['w_ukT'], lw['inv_full']]
    in_specs = [row(d)] + [_const_spec(c.shape) for c in consts]
    qcat_shape = (nb, N_HEADS, tm * tps, QK_W)
    qcat_spec = pl.BlockSpec((1, N_HEADS, tm, QK_W), lambda i: (i // tps, 0, i % tps, 0))
    sds = jax.ShapeDtypeStruct
    common_out = [
        (sds((m, RWKV_SHIFT_W), F32), row(RWKV_SHIFT_W)),
        (sds((m, GROUP_W), F32), row(GROUP_W)),
        (sds((m, KV_RANK), F32), row(KV_RANK)),
        (sds((m, ROPE_DIM), F32), row(ROPE_DIM)),
        (sds((m, QK_W), BF16), row(QK_W)),
        (sds(qcat_shape, BF16), qcat_spec),
        (sds((m, GROUP_W), F32), row(GROUP_W)),
    ]
    if decode:
        ins = [x] + consts + [conv_state[:, 0], conv_state[:, 1]]
        in_specs += [row(GROUP_W), row(GROUP_W)]
        outs = ([(sds((m, GROUP_W), BF16), row(GROUP_W)),
                 (sds((m, GROUP_W), F32), row(GROUP_W)), (sds((m, GROUP_W), F32), row(GROUP_W))]
                + common_out
                + [(sds((m, GROUP_W), F32), row(GROUP_W)), (sds((m, GROUP_W), F32), row(GROUP_W))])
        scratch = []
    else:
        ins = [x] + consts + [mk, mv]
        n_mem = mk.shape[1]
        in_specs += [pl.BlockSpec((1, n_mem, GROUP_W), lambda i: (i // tps, 0, 0))] * 2
        outs = ([(sds((m, GROUP_W), BF16), row(GROUP_W)),
                 (sds((nb, CONV_K - 1, GROUP_W), F32),
                  pl.BlockSpec((1, CONV_K - 1, GROUP_W), lambda i: (i // tps, 0, 0)))]
                + common_out
                + [(sds((m, GROUP_W), BF16), row(GROUP_W))])
        scratch = [pltpu.VMEM((8, GROUP_W), F32), pltpu.VMEM((tm, LANES), F32), pltpu.VMEM((tm, LANES), F32)]
    return pl.pallas_call(
        functools.partial(_proj_body, decode, tps, pos0),
        grid=grid, in_specs=in_specs,
        out_specs=[o[1] for o in outs], out_shape=[o[0] for o in outs],
        scratch_shapes=scratch, compiler_params=_params(("arbitrary",)),
        name="proj_decode" if decode else "proj_prompt",
    )(*ins)


def _memkv_body(x_ref, g_ref, w_ref, k_ref, v_ref, kb_ref, vb_ref):
    kv = _mm(_rms(x_ref[...], g_ref[...]), w_ref[...])
    k, v = kv[:, :GROUP_W], kv[:, GROUP_W:]
    k_ref[...] = k
    v_ref[...] = v
    kb_ref[...] = k.astype(BF16)
    vb_ref[...] = v.astype(BF16)


def _memkv_call(mem, g, w_kv):
    m, d = mem.shape
    tm = _tile(m, PROJ_ROWS)
    row = lambda w: pl.BlockSpec((tm, w), lambda i: (i, 0))
    sds = jax.ShapeDtypeStruct
    return pl.pallas_call(
        _memkv_body, grid=(m // tm,),
        in_specs=[row(d), _const_spec(g.shape), _const_spec(w_kv.shape)],
        out_specs=[row(GROUP_W)] * 4,
        out_shape=[sds((m, GROUP_W), F32), sds((m, GROUP_W), F32), sds((m, GROUP_W), BF16), sds((m, GROUP_W), BF16)],
        compiler_params=_params(("arbitrary",)), name="mem_kv",
    )(mem, g, w_kv)


def _head_ones():
    r = lax.broadcasted_iota(jnp.int32, (GROUP_W, GROUP_W), 0) // HEAD_DIM
    c = lax.broadcasted_iota(jnp.int32, (GROUP_W, GROUP_W), 1) // HEAD_DIM
    return (r == c).astype(F32)


def _rwkv_tokens(mixed, prm, ones_bd):
    w0, a0, k_k, k_a, r_k = (prm[i:i + 1, :] for i in range(5))
    w2p, a2p = prm[8:8 + LANES, :], prm[8 + LANES:8 + 2 * LANES, :]
    r, k, v = mixed[:, 0:GROUP_W], mixed[:, GROUP_W:2 * GROUP_W], mixed[:, 2 * GROUP_W:3 * GROUP_W]
    wa = mixed[:, 3 * GROUP_W:RWKV_SHIFT_W]
    logw = -DECAY_SCALE * _sigmoid(w0 + _mm_hi(jnp.tanh(wa), w2p))
    a = _sigmoid(a0 + _mm_hi(wa, a2p))
    kkr = k * k_k
    kk = kkr * lax.rsqrt(_mm(kkr * kkr, ones_bd) + 1e-12)
    k2 = k * (1.0 + (a - 1.0) * k_a)
    bonus = _mm(r * k2 * r_k, ones_bd) * v
    return r, k2, v, logw, kk, a, bonus


def _rwkv_finish(y, bonus, gate, prm, ones_bd):
    lnx_g, lnx_b = prm[5:6, :], prm[6:7, :]
    mean = _mm(y, ones_bd) * (1.0 / HEAD_DIM)
    yc = y - mean
    var = _mm(yc * yc, ones_bd) * (1.0 / HEAD_DIM)
    return ((yc * lax.rsqrt(var + LNX_EPS)) * lnx_g + lnx_b + bonus) * gate


def _stack_heads(x, head_lane):
    return jnp.concatenate([jnp.where(head_lane == hh, x, 0.0) for hh in range(N_HEADS)], axis=0)


def _unstack_heads(x):
    c = x.shape[0] // N_HEADS
    return x[0:c] + x[c:2 * c] + x[2 * c:3 * c] + x[3 * c:4 * c]


def _rwkv_prompt_body(x_ref, gate_ref, mu_ref, prm_ref, o_ref, st_ref, prev_ref, s_ref):
    ct = x_ref.shape[1]
    si = pl.program_id(1)

    @pl.when(si == 0)
    def _():
        prev_ref[...] = jnp.zeros_like(prev_ref)
        s_ref[...] = jnp.zeros_like(s_ref)

    x = x_ref[0]
    row = lax.broadcasted_iota(jnp.int32, x.shape, 0)
    prev = jnp.where(row == 0, prev_ref[0:1, :], pltpu.roll(x, 1, 0))
    prev_ref[0:1, :] = x[ct - 1:ct, :]
    mixed = x + (prev - x) * mu_ref[...]
    prm = prm_ref[...]
    ones_bd = _head_ones()
    r, k2, v, logw, kk, a, bonus = _rwkv_tokens(mixed, prm, ones_bd)
    b = kk * a

    n = N_HEADS * CHUNK
    head_lane = lax.broadcasted_iota(jnp.int32, (CHUNK, GROUP_W), 1) // HEAD_DIM
    rr = lax.broadcasted_iota(jnp.int32, (n, n), 0)
    cc = lax.broadcasted_iota(jnp.int32, (n, n), 1)
    same_sub = (rr // SUB) == (cc // SUB)
    same_head = (rr // CHUNK) == (cc // CHUNK)
    eye = (rr == cc).astype(F32)
    t_w = lax.broadcasted_iota(jnp.int32, (CHUNK, GROUP_W), 0)
    s_w = lax.broadcasted_iota(jnp.int32, (CHUNK, GROUP_W), 1) % CHUNK
    strict_w, incl_w = t_w > s_w, t_w >= s_w
    eye_w = (t_w == s_w).astype(F32)
    eye_sub = (lax.broadcasted_iota(jnp.int32, (SUB, GROUP_W), 0)
               == lax.broadcasted_iota(jnp.int32, (SUB, GROUP_W), 1) % SUB).astype(F32)
    tri = (lax.broadcasted_iota(jnp.int32, (CHUNK, CHUNK), 0)
           >= lax.broadcasted_iota(jnp.int32, (CHUNK, CHUNK), 1)).astype(F32)

    def sub_square(z):
        return jnp.where(same_sub, jnp.concatenate([z] * (n // SUB), axis=0), 0.0)

    def sub_strip(z):
        out = z[0:SUB]
        for i in range(1, n // SUB):
            out = out + z[i * SUB:(i + 1) * SUB]
        return out

    chunks = range(ct // CHUNK)
    each = lambda f, *xs: [f(*(x[c] for x in xs)) for c in chunks]
    rows = lambda z: [z[c * CHUNK:(c + 1) * CHUNK] for c in chunks]
    stack = lambda z: _stack_heads(z, head_lane)
    lw = rows(logw)
    cw = each(lambda l: _mm_exact_lhs(tri, l), lw)
    cwl = each(lambda z: z[CHUNK - 1:CHUNK, :], cw)
    e_ng = each(lambda z: jnp.exp(-z), cw)
    e_w = each(lambda z, zl: jnp.exp(zl - z), cw, cwl)
    kkd = each(lambda x, z, l: x * jnp.exp(z - l), rows(kk), cw, lw)
    rd = each(lambda x, z: x * jnp.exp(z), rows(r), cw)
    kinvw = each(lambda x, e: x * e, rows(k2), e_w)
    binvw = each(lambda x, e: x * e, rows(b), e_w)
    v_w = rows(v)
    kkd_sq = each(stack, kkd)
    kinv_sq = each(lambda x, e: stack(x * e), rows(k2), e_ng)
    binv_sq = each(lambda x, e: stack(x * e), rows(b), e_ng)
    v_sq = each(stack, v_w)
    kr_rows = each(lambda x, y: jnp.concatenate([x, y], axis=0), kkd, rd)
    ab = each(_mm_nt, kr_rows, binv_sq)
    ak = each(_mm_nt, kr_rows, kinv_sq)
    a_bk = each(lambda z: jnp.where(strict_w, z[0:CHUNK], 0.0), ab)
    m_rb = each(lambda z: jnp.where(incl_w, z[CHUNK:2 * CHUNK], 0.0), ab)
    a_vk = each(lambda z: jnp.where(strict_w, z[0:CHUNK], 0.0), ak)
    m_rk = each(lambda z: jnp.where(incl_w, z[CHUNK:2 * CHUNK], 0.0), ak)
    a_sq = each(stack, a_bk)
    d1_sq = each(lambda x: jnp.where(same_sub, x, 0.0), a_sq)
    e1_sq = each(lambda x, y: x - y, a_sq, d1_sq)
    d1 = each(sub_strip, d1_sq)
    d2 = each(_mm, d1, d1_sq)
    t0 = each(lambda x: eye_sub - x, d1)
    r2 = each(lambda x, y, z: _mm(jnp.concatenate([x, y], axis=0), sub_square(z)), d2, t0, d2)
    d4 = each(lambda z: z[0:SUB], r2)
    t1 = each(lambda x, z: x + z[SUB:2 * SUB], t0, r2)
    r4 = each(lambda x, y, z: _mm(jnp.concatenate([x, y], axis=0), sub_square(z)), d4, t1, d4)
    d8 = each(lambda z: z[0:SUB], r4)
    t2 = each(lambda x, z: x + z[SUB:2 * SUB], t1, r4)
    dinv = each(lambda x, y: x + _mm(x, sub_square(y)), t2, d8)
    dinv_sq = each(sub_square, dinv)
    dinv_w = each(_unstack_heads, dinv_sq)
    nn = each(_mm, dinv_w, e1_sq)
    nn_sq = each(stack, nn)
    n2 = each(_mm, nn, nn_sq)
    t3 = each(lambda x, y: _mm(eye_w - x, eye + stack(y)), nn, n2)
    tinv = each(_mm, t3, dinv_sq)
    xv = each(lambda x, y, z: _mm(jnp.concatenate([x, y], axis=0), z), a_vk, m_rk, v_sq)
    x_w = each(lambda z: z[0:CHUNK], xv)
    kkdp = each(_mm, tinv, kkd_sq)
    vp = each(_mm, tinv, each(stack, x_w))
    kkdp_sq = each(stack, kkdp)
    rq = each(lambda x, y, z: x - _mm(y, z), rd, m_rb, kkdp_sq)
    y_in = each(lambda z, a3, a4: z[CHUNK:2 * CHUNK] - _mm(a3, stack(a4)), xv, m_rb, vp)
    phi = each(lambda zl, x, y: eye * jnp.exp(zl) - jnp.where(same_head, _mm_tn(x, y), 0.0), cwl, kkdp, binvw)
    g = each(lambda a1, a2, a3, a4: _unstack_heads(jnp.where(
        same_head, _mm_tn(jnp.concatenate([a1, -a3], axis=0), jnp.concatenate([a2, a4], axis=0)), 0.0)),
        v_w, kinvw, vp, binvw)
    state = s_ref[...]
    ys = []
    for c in chunks:
        ys.append(_mm_nt(rq[c], stack(state)) + y_in[c])
        state = _mm(state, phi[c]) + g[c]
    s_ref[...] = state
    st_ref[0] = state
    y = jnp.concatenate(ys, axis=0) if len(ys) > 1 else ys[0]
    o_ref[...] = _rwkv_finish(y, bonus, gate_ref[...], prm, ones_bd).astype(o_ref.dtype)


def _rwkv_prompt_call(r_sh, gate, lw, nb, seq_len):
    ct = _tile(seq_len, RWKV_ROWS)
    ns = seq_len // ct
    x3 = r_sh.reshape(nb, seq_len, RWKV_SHIFT_W)
    sds = jax.ShapeDtypeStruct
    out, st = pl.pallas_call(
        _rwkv_prompt_body, grid=(nb, ns),
        in_specs=[pl.BlockSpec((1, ct, RWKV_SHIFT_W), lambda b, s: (b, s, 0)),
                  pl.BlockSpec((ct, GROUP_W), lambda b, s: (b * ns + s, 0)),
                  _const_spec(lw['mu'].shape), _const_spec(lw['rwkv_prm'].shape)],
        out_specs=[pl.BlockSpec((ct, GROUP_W), lambda b, s: (b * ns + s, 0)),
                   pl.BlockSpec((1, HEAD_DIM, GROUP_W), lambda b, s: (b, 0, 0))],
        out_shape=[sds((nb * seq_len, GROUP_W), BF16), sds((nb, HEAD_DIM, GROUP_W), F32)],
        scratch_shapes=[pltpu.VMEM((8, RWKV_SHIFT_W), F32),
                        pltpu.VMEM((HEAD_DIM, GROUP_W), F32)],
        compiler_params=_params(("arbitrary", "arbitrary")), name="rwkv_prompt",
    )(x3, gate, lw['mu'], lw['rwkv_prm'])
    st = st.reshape(nb, HEAD_DIM, N_HEADS, HEAD_DIM).transpose(0, 2, 1, 3)
    return out, st


def _rwkv_step_body(x_ref, prev_ref, gate_ref, mu_ref, prm_ref, s_ref, o_ref, so_ref, yt_ref, vt_ref):
    hh = pl.program_id(0)
    x = x_ref[...]
    mixed = x + (prev_ref[...] - x) * mu_ref[...]
    prm = prm_ref[...]
    ones_bd = _head_ones()
    r, k2, v, logw, kk, a, bonus = _rwkv_tokens(mixed, prm, ones_bd)
    row0 = pl.multiple_of(hh * HEAD_DIM, HEAD_DIM)
    for i, z in enumerate((kk, jnp.exp(logw), kk * a, v, k2, r)):
        vt_ref[i] = z.T
    head = lambda i: vt_ref[i, pl.ds(row0, HEAD_DIM), :]
    st = s_ref[...]
    sa = jnp.sum(st * head(0)[None, :, :], axis=1)
    st = st * head(1)[None, :, :] - sa[:, None, :] * head(2)[None, :, :] + head(3)[:, None, :] * head(4)[None, :, :]
    so_ref[...] = st
    yt_ref[pl.ds(row0, HEAD_DIM), :] = jnp.sum(st * head(5)[None, :, :], axis=1)

    @pl.when(hh == N_HEADS - 1)
    def _():
        o_ref[...] = _rwkv_finish(yt_ref[...].T, bonus, gate_ref[...], prm, ones_bd).astype(o_ref.dtype)


def _rwkv_step_call(layer, r_sh, prev, gate, state_t, lw):
    m = r_sh.shape[0]
    full = lambda w: pl.BlockSpec((m, w), lambda h: (0, 0))
    sds = jax.ShapeDtypeStruct
    return pl.pallas_call(
        _rwkv_step_body, grid=(N_HEADS,),
        in_specs=[full(RWKV_SHIFT_W), full(RWKV_SHIFT_W), full(GROUP_W),
                  _const_spec(lw['mu'].shape), _const_spec(lw['rwkv_prm'].shape),
                  pl.BlockSpec((None, None, HEAD_DIM, HEAD_DIM, m), lambda h: (layer, h, 0, 0, 0))],
        out_specs=[full(GROUP_W), pl.BlockSpec((None, HEAD_DIM, HEAD_DIM, m), lambda h: (h, 0, 0, 0))],
        out_shape=[sds((m, GROUP_W), BF16), sds((N_HEADS, HEAD_DIM, HEAD_DIM, m), F32)],
        scratch_shapes=[pltpu.VMEM((GROUP_W, m), F32), pltpu.VMEM((6, GROUP_W, m), F32)],
        compiler_params=_params(("arbitrary",)), name="rwkv_step",
    )(r_sh, prev, gate, lw['mu'], lw['rwkv_prm'], state_t)


def _softmax_update(s, v, m_sc, l_sc, acc_sc):
    m_prev = m_sc[...]
    m_new = jnp.maximum(m_prev, jnp.max(s, axis=-1, keepdims=True))
    alpha = jnp.exp(m_prev - m_new)
    p = jnp.exp(s - m_new)
    l_sc[...] = alpha * l_sc[...] + jnp.sum(p, axis=-1, keepdims=True)
    acc_sc[...] = alpha * acc_sc[...] + _mm(p, v)
    m_sc[...] = m_new


def _flash_body(q_ref, k_ref, gate_ref, wuv_ref, x_ref, mconv_ref, mrwkv_ref, mmem_ref, wout_ref, pg_ref,
                o_ref, sa_ref, sb_ref, m_ref, acc_ref):
    qi = pl.program_id(1)
    tq = q_ref.shape[2]

    def keys(kb):
        return k_ref[0, pl.ds(pl.multiple_of(kb * tq, tq), tq), :]

    def scores(kb, s_ref):
        k = keys(kb)
        for hh in range(N_HEADS):
            s_ref[hh] = _mm_nt(q_ref[0, hh], k)

    def update(s_ref, kb, masked):
        k = keys(kb)
        for hh in range(N_HEADS):
            s_h = s_ref[hh]
            if masked:
                s_h = jnp.where(lax.broadcasted_iota(jnp.int32, s_h.shape, 1)
                                <= lax.broadcasted_iota(jnp.int32, s_h.shape, 0), s_h, NEG_INF)
            m_prev = m_ref[hh]
            m_new = jnp.maximum(m_prev, jnp.max(s_h, axis=-1, keepdims=True))
            p = jnp.exp(s_h - jnp.concatenate([m_new] * (tq // LANES), axis=1))
            alpha = jnp.exp(m_prev - m_new)
            acc_ref[hh] = jnp.concatenate([alpha] * (QK_W // LANES), axis=1) * acc_ref[hh] + _mm(p, k)
            m_ref[hh] = m_new

    def finish(s_ref):
        update(s_ref, qi, True)
        mla = jnp.zeros((tq, GROUP_W), F32)
        for hh in range(N_HEADS):
            acc = acc_ref[hh]
            o = acc[:, 0:KV_RANK] / acc[:, ONES_LANE:ONES_LANE + 1]
            mla = mla + _mm(o, wuv_ref[hh])
        mixed = (mconv_ref[...], mrwkv_ref[...], (mla * gate_ref[...]).astype(BF16), mmem_ref[...])
        y = jnp.zeros(x_ref.shape, F32)
        for i, m_i in enumerate(mixed):
            y = y + _mm(m_i, wout_ref[i * GROUP_W:(i + 1) * GROUP_W, :])
        o_ref[...] = x_ref[...] + _rms(y, pg_ref[...])

    m_ref[...] = jnp.full_like(m_ref, NEG_INF)
    acc_ref[...] = jnp.zeros_like(acc_ref)
    scores(0, sa_ref)

    def pair(j, _):
        kb = 2 * j
        scores(kb + 1, sb_ref)
        update(sa_ref, kb, False)
        scores(kb + 2, sa_ref)
        update(sb_ref, kb + 1, False)
        return 0

    lax.fori_loop(0, qi // 2, pair, 0)

    @pl.when(qi % 2 == 1)
    def _():
        scores(qi, sb_ref)
        update(sa_ref, qi - 1, False)
        finish(sb_ref)

    @pl.when(qi % 2 == 0)
    def _():
        finish(sa_ref)


def _flash_call(qcat, kcat, gate, w_uv_pad, x, m_conv, m_rwkv, m_mem, w_out, post_g, seq_len):
    nb = qcat.shape[0]
    d = x.shape[1]
    tq = _tile(seq_len, FLASH_ROWS)
    nq = seq_len // tq
    k3 = kcat.reshape(nb, seq_len, QK_W)
    row = lambda w: pl.BlockSpec((tq, w), lambda b, i: (b * nq + i, 0))
    return pl.pallas_call(
        _flash_body, grid=(nb, nq),
        in_specs=[pl.BlockSpec((1, N_HEADS, tq, QK_W), lambda b, i: (b, 0, i, 0)),
                  pl.BlockSpec((1, seq_len, QK_W), lambda b, i: (b, 0, 0)),
                  row(GROUP_W), _const_spec(w_uv_pad.shape),
                  row(d), row(GROUP_W), row(GROUP_W), row(GROUP_W),
                  _const_spec(w_out.shape), _const_spec(post_g.shape)],
        out_specs=row(d),
        out_shape=jax.ShapeDtypeStruct((nb * seq_len, d), F32),
        scratch_shapes=[pltpu.VMEM((N_HEADS, tq, tq), F32), pltpu.VMEM((N_HEADS, tq, tq), F32),
                        pltpu.VMEM((N_HEADS, tq, LANES), F32), pltpu.VMEM((N_HEADS, tq, QK_W), F32)],
        compiler_params=_params(("arbitrary", "arbitrary")), name="flash_mla",
    )(qcat, k3, gate, w_uv_pad, x, m_conv, m_rwkv, m_mem, w_out, post_g)


def _decode_body(layer, nch, pg, pt_ref, q_ref, cnew_ref, knew_ref, gate_ref, wuv_ref, ckv_hbm, kr_hbm,
                 o_ref, cbuf, kbuf, sem, m_sc, l_sc, acc_sc):
    b, c = pl.program_id(0), pl.program_id(1)
    step = b * nch + c
    total = pl.num_programs(0) * nch
    slot = step % 2
    page_len = cbuf.shape[2]

    def page_copies(st, sl):
        bb, cc = st // nch, st % nch
        out = []
        for p in range(pg):
            page = pt_ref[bb, cc * pg + p]
            out.append(pltpu.make_async_copy(ckv_hbm.at[layer, page], cbuf.at[sl, p], sem.at[0, sl]))
            out.append(pltpu.make_async_copy(kr_hbm.at[layer, page], kbuf.at[sl, :, pl.ds(p * page_len, page_len)],
                                             sem.at[1, sl]))
        return out

    def start_all(copies):
        for i, cp in enumerate(copies):
            cp.start(priority=(i // 2) % 2)

    @pl.when(step == 0)
    def _():
        start_all(page_copies(0, 0))

    @pl.when(step + 1 < total)
    def _():
        start_all(page_copies(step + 1, 1 - slot))

    @pl.when(c == 0)
    def _():
        m_sc[...] = jnp.full_like(m_sc, NEG_INF)
        l_sc[...] = jnp.zeros_like(l_sc)
        acc_sc[...] = jnp.zeros_like(acc_sc)

    for cp in page_copies(step, slot):
        cp.wait()

    ck = cbuf[slot].reshape(pg * page_len, KV_RANK).astype(BF16)
    q = q_ref[0]
    s = _mm_nt(q[:, 0:KV_RANK], ck) + _mm(q[:, KV_RANK:QK_DIM], kbuf[slot])
    _softmax_update(s, ck, m_sc, l_sc, acc_sc)

    @pl.when(c == nch - 1)
    def _():
        qf = q.astype(F32)
        cn, kn = cnew_ref[0], knew_ref[0]
        s_new = (jnp.sum(qf[:, 0:KV_RANK] * cn, axis=-1, keepdims=True)
                 + jnp.sum(qf[:, KV_RANK:QK_DIM] * kn, axis=-1, keepdims=True))
        m_prev = m_sc[...]
        m_new = jnp.maximum(m_prev, s_new)
        alpha, p_new = jnp.exp(m_prev - m_new), jnp.exp(s_new - m_new)
        l_fin = alpha * l_sc[...] + p_new
        o = (alpha * acc_sc[...] + p_new * cn) / l_fin
        out = jnp.zeros((1, GROUP_W), F32)
        for hh in range(N_HEADS):
            out = out + _mm(o, wuv_ref[hh])[hh:hh + 1, :]
        o_ref[0] = out * gate_ref[0]


def _decode_call(layer, page_table, q16, ckv_new, kr_new, gate, w_uv_pad, cache_ckv, cache_krope):
    nb, n_pages = page_table.shape
    page = cache_ckv.shape[2]
    pg = _tile(n_pages, DECODE_PAGES)
    nch = n_pages // pg
    blk = lambda shape: pl.BlockSpec((1,) + shape, lambda b, c, pt: (b, 0, 0))
    grid_spec = pltpu.PrefetchScalarGridSpec(
        num_scalar_prefetch=1, grid=(nb, nch),
        in_specs=[blk((16, QK_W)), blk((1, KV_RANK)), blk((1, ROPE_DIM)), blk((1, GROUP_W)),
                  pl.BlockSpec(w_uv_pad.shape, lambda b, c, pt: (0, 0, 0)),
                  pl.BlockSpec(memory_space=pl.ANY), pl.BlockSpec(memory_space=pl.ANY)],
        out_specs=blk((1, GROUP_W)),
        scratch_shapes=[pltpu.VMEM((2, pg, page, KV_RANK), F32), pltpu.VMEM((2, ROPE_DIM, pg * page), F32),
                        pltpu.SemaphoreType.DMA((2, 2)),
                        pltpu.VMEM((16, 1), F32), pltpu.VMEM((16, 1), F32), pltpu.VMEM((16, KV_RANK), F32)])
    out = pl.pallas_call(
        functools.partial(_decode_body, layer, nch, pg), grid_spec=grid_spec,
        out_shape=jax.ShapeDtypeStruct((nb, 1, GROUP_W), F32),
        compiler_params=_params(("arbitrary", "arbitrary")), name="paged_decode",
    )(page_table, q16, ckv_new[:, None, :], kr_new[:, None, :], gate[:, None, :], w_uv_pad,
      cache_ckv, cache_krope)
    return out.reshape(nb, GROUP_W)


def _memattn_body(q_ref, gate_ref, k_ref, v_ref, o_ref):
    tb = q_ref.shape[0]
    head = lax.broadcasted_iota(jnp.int32, (8, GROUP_W), 1) // HEAD_DIM
    sel = head == lax.broadcasted_iota(jnp.int32, (8, GROUP_W), 0)
    for bi in range(tb):
        q_bd = jnp.where(sel, q_ref[bi], 0.0)
        s = _mm(q_bd, k_ref[0, bi])
        p = jnp.exp(s - jnp.max(s, axis=-1, keepdims=True))
        p = p / jnp.sum(p, axis=-1, keepdims=True)
        o = jnp.where(sel, _mm_nt(p, v_ref[0, bi]), 0.0)
        o_ref[bi] = jnp.sum(o, axis=0, keepdims=True) * gate_ref[bi]


def _memattn_call(layer, mem_q, gate, mem_k, mem_v):
    m = mem_q.shape[0]
    n_mem = mem_k.shape[3]
    tb = _tile(m, MEMATTN_TOKENS)
    vec = pl.BlockSpec((tb, 1, GROUP_W), lambda i: (i, 0, 0))
    kv = pl.BlockSpec((1, tb, GROUP_W, n_mem), lambda i: (layer, i, 0, 0))
    out = pl.pallas_call(
        _memattn_body, grid=(m // tb,),
        in_specs=[vec, vec, kv, kv], out_specs=vec,
        out_shape=jax.ShapeDtypeStruct((m, 1, GROUP_W), F32),
        compiler_params=_params(("arbitrary",)), name="mem_attn_decode",
    )(mem_q[:, None, :], gate[:, None, :], mem_k, mem_v)
    return out.reshape(m, GROUP_W)


def _outproj_body(x_ref, m0_ref, m1_ref, m2_ref, m3_ref, w_ref, g_ref, o_ref):
    acc = jnp.zeros(x_ref.shape, F32)
    for i, m_ref in enumerate((m0_ref, m1_ref, m2_ref, m3_ref)):
        acc = acc + _mm(m_ref[...], w_ref[i * GROUP_W:(i + 1) * GROUP_W, :])
    o_ref[...] = x_ref[...] + _rms(acc, g_ref[...])


def _outproj_call(x, mixed, w_out, g):
    m, d = x.shape
    tm = _tile(m, PROJ_ROWS)
    row = lambda w: pl.BlockSpec((tm, w), lambda i: (i, 0))
    return pl.pallas_call(
        _outproj_body, grid=(m // tm,),
        in_specs=[row(d)] + [row(GROUP_W)] * 4 + [_const_spec(w_out.shape), _const_spec(g.shape)],
        out_specs=row(d), out_shape=jax.ShapeDtypeStruct((m, d), F32),
        compiler_params=_params(("arbitrary",)), name="out_proj",
    )(x, *mixed, w_out, g)


def _prep_layer(l, pre_norm_g, post_norm_g, w_in, w_out, conv_w, rwkv_mu, rwkv_w0, rwkv_w2, rwkv_a0, rwkv_a2,
                rwkv_k_k, rwkv_k_a, rwkv_r_k, rwkv_lnx_g, rwkv_lnx_b, mla_q_norm_g, mla_w_uq, mla_kv_norm_g,
                mla_w_uk, mla_w_uv, mem_norm_g, w_mem_k, w_mem_v):
    d = w_in.shape[1]
    kr0 = _C_MLAG
    w = w_in[l]
    w_in_p = jnp.concatenate([w[:, :kr0], w[:, kr0 + ROPE_DIM:], w[:, kr0:kr0 + ROPE_DIM],
                              jnp.zeros((d, LANES - ROPE_DIM), F32)], axis=1).astype(BF16)
    uq = mla_w_uq[l].reshape(Q_RANK, N_HEADS, NOPE_DIM + ROPE_DIM)
    w_uq = jnp.concatenate([uq[:, :, :NOPE_DIM].reshape(Q_RANK, -1), uq[:, :, NOPE_DIM:].reshape(Q_RANK, -1)],
                           axis=1).astype(BF16)
    w_uv_pad = jnp.zeros((N_HEADS, KV_RANK, GROUP_W), F32)
    for hh in range(N_HEADS):
        w_uv_pad = w_uv_pad.at[hh, :, hh * HEAD_DIM:(hh + 1) * HEAD_DIM].set(mla_w_uv[l][:, hh, :])
    w_uk_bd = jnp.zeros((N_HEADS * NOPE_DIM, N_HEADS * KV_RANK), F32)
    for hh in range(N_HEADS):
        w_uk_bd = w_uk_bd.at[hh * NOPE_DIM:(hh + 1) * NOPE_DIM, hh * KV_RANK:(hh + 1) * KV_RANK].set(
            mla_w_uk[l][:, hh, :].T)
    half = ROPE_DIM // 2
    inv = jnp.power(ROPE_BASE, -jnp.arange(half, dtype=F32) / half)
    zl = jnp.zeros((LORA_W, GROUP_W), F32)
    vecs = jnp.stack([rwkv_w0[l], rwkv_a0[l], rwkv_k_k[l], rwkv_k_a[l], rwkv_r_k[l].reshape(-1),
                      rwkv_lnx_g[l], rwkv_lnx_b[l], jnp.zeros((GROUP_W,), F32)])
    return {
        'pre_g': pre_norm_g[l][None], 'post_g': post_norm_g[l][None], 'w_in': w_in_p,
        'w_out': w_out[l].astype(BF16), 'conv_w': conv_w[l], 'q_norm_g': mla_q_norm_g[l][None], 'w_uq': w_uq,
        'kv_norm_g': mla_kv_norm_g[l][None], 'w_ukT': w_uk_bd.astype(BF16),
        'w_uv_pad': w_uv_pad.astype(BF16), 'inv_full': jnp.tile(inv, LANES // half)[None],
        'mu': rwkv_mu[l][None],
        'rwkv_prm': jnp.concatenate([vecs, rwkv_w2[l], zl, zl, rwkv_a2[l]], axis=0),
        'mem_g': mem_norm_g[l][None],
        'w_mem_kv': jnp.concatenate([w_mem_k[l], w_mem_v[l]], axis=1).astype(BF16),
    }


def kernel(x_prompt, x_sample, cache_ckv, cache_krope, cache_mem_k, cache_mem_v, state_conv, state_rwkv_shift, state_rwkv, page_table, mem_prompt, pre_norm_g, post_norm_g, w_in, w_out, conv_w, rwkv_mu, rwkv_w0, rwkv_w2, rwkv_a0, rwkv_a2, rwkv_k_k, rwkv_k_a, rwkv_r_k, rwkv_lnx_g, rwkv_lnx_b, mla_q_norm_g, mla_w_uq, mla_kv_norm_g, mla_w_uk, mla_w_uv, mem_norm_g, w_mem_k, w_mem_v):
    bp, sp, d = x_prompt.shape
    bs, ts, _ = x_sample.shape
    assert ts == 1
    depth = w_in.shape[0]
    n_mem = mem_prompt.shape[1]
    past_len = page_table.shape[1] * cache_ckv.shape[2]
    mem_k4 = jnp.transpose(cache_mem_k, (0, 1, 3, 4, 2)).reshape(depth, bs, GROUP_W, n_mem)
    mem_v4 = jnp.transpose(cache_mem_v, (0, 1, 3, 4, 2)).reshape(depth, bs, GROUP_W, n_mem)
    cache_krope_t = jnp.swapaxes(cache_krope, 2, 3)
    state_rwkv_t = jnp.transpose(state_rwkv, (0, 2, 3, 4, 1))
    xp = x_prompt.reshape(bp * sp, d)
    xs = x_sample.reshape(bs, d)
    mem2 = mem_prompt.reshape(bp * n_mem, d)
    outs = [[] for _ in range(12)]
    for l in range(depth):
        lw = _prep_layer(l, pre_norm_g, post_norm_g, w_in, w_out, conv_w, rwkv_mu, rwkv_w0, rwkv_w2, rwkv_a0,
                         rwkv_a2, rwkv_k_k, rwkv_k_a, rwkv_r_k, rwkv_lnx_g, rwkv_lnx_b, mla_q_norm_g, mla_w_uq,
                         mla_kv_norm_g, mla_w_uk, mla_w_uv, mem_norm_g, w_mem_k, w_mem_v)
        mk, mv, mk_b, mv_b = _memkv_call(mem2, lw['mem_g'], lw['w_mem_kv'])
        (m_conv, conv_p, r_sh, r_gate, ckv, kr, kcat, qcat, mla_gate, m_mem) = _proj_call(
            xp, lw, decode=False, seq_len=sp,
            mk=mk_b.reshape(bp, n_mem, GROUP_W), mv=mv_b.reshape(bp, n_mem, GROUP_W))
        m_rwkv, st_p = _rwkv_prompt_call(r_sh, r_gate, lw, bp, sp)
        xp = _flash_call(qcat, kcat, mla_gate, lw['w_uv_pad'], xp, m_conv, m_rwkv, m_mem, lw['w_out'],
                         lw['post_g'], sp)
        sh_p = r_sh.reshape(bp, sp, RWKV_SHIFT_W)[:, -1]
        (s_conv, cn0, cn1, r_sh_s, r_gate_s, ckv_s, kr_s, _, qcat_s, mla_gate_s, mem_q_s, mem_gate_s) = _proj_call(
            xs, lw, decode=True, seq_len=1, pos0=float(past_len), conv_state=state_conv[l])
        s_rwkv, st_s = _rwkv_step_call(l, r_sh_s, state_rwkv_shift[l], r_gate_s, state_rwkv_t, lw)
        q16 = jnp.pad(jnp.transpose(qcat_s[0], (1, 0, 2)), ((0, 0), (0, 16 - N_HEADS), (0, 0)))
        s_mla = _decode_call(l, page_table, q16, ckv_s, kr_s, mla_gate_s, lw['w_uv_pad'], cache_ckv, cache_krope_t)
        s_mem = _memattn_call(l, mem_q_s, mem_gate_s, mem_k4, mem_v4)
        xs = _outproj_call(xs, (s_conv, s_rwkv, s_mla, s_mem), lw['w_out'], lw['post_g'])
        vals = (ckv.reshape(bp, sp, KV_RANK), ckv_s.reshape(bs, ts, KV_RANK),
                kr.reshape(bp, sp, ROPE_DIM), kr_s.reshape(bs, ts, ROPE_DIM),
                mk.reshape(bp, n_mem, N_HEADS, HEAD_DIM), mv.reshape(bp, n_mem, N_HEADS, HEAD_DIM),
                conv_p, jnp.stack([cn0, cn1], axis=1), sh_p, r_sh_s, st_p, st_s)
        for o, v in zip(outs, vals):
            o.append(v)
    res = [jnp.stack(o) for o in outs]
    res[-1] = jnp.transpose(res[-1], (0, 4, 1, 2, 3))
    return (xp.reshape(bp, sp, d), xs.reshape(bs, ts, d)) + tuple(res)
```
